```python
import math, functools
import jax, jax.numpy as jnp
from jax import lax
import numpy as np

D_MODEL = 4096
BATCH = 1
SEQ = 8192
DEPTH = 1
DEC_BATCH = 32
DEC_SEQ = 8
PAST_LEN = 8192
PAGE_SIZE = 128

ATTN_HEADS = 16
KV_HEADS = 4
HEAD_DIM = 128
GQA_GROUPS = ATTN_HEADS // KV_HEADS
ATTN_WIDTH = ATTN_HEADS * HEAD_DIM
KV_WIDTH = KV_HEADS * HEAD_DIM
IDX_HEADS = 32
IDX_DIM = 64
INDEX_TOPK = 256
Q_BLOCK = 128
DN_HEADS = 16
DN_DK = 128
DN_DV = 128
DN_WIDTH = DN_HEADS * DN_DV
DN_CONV_CH = DN_HEADS * (2 * DN_DK + DN_DV)
CONV_W = 4
DN_CHUNK = 64
N_EXPERTS = 128
TOP_K = 8
N_GROUPS = 8
TOPK_GROUPS = 4
D_EXPERT = 1024
D_SHARED = 1024
ROUTED_SCALE = 2.5
MOE_BLOCK = 128
EPS = 1e-6
IN_WIDTHS = (ATTN_WIDTH, KV_WIDTH, KV_WIDTH, IDX_HEADS * IDX_DIM, IDX_DIM, IDX_HEADS, DN_CONV_CH, DN_WIDTH, DN_HEADS, DN_HEADS, D_MODEL, D_MODEL)
N_IN = sum(IN_WIDTHS)

kernel_name = 'hybrid_dsa_gdn_moe_step'


def rms_norm(x, g):
    xf = x.astype(jnp.float32)
    y = xf * lax.rsqrt(jnp.mean(xf * xf, axis=-1, keepdims=True) + EPS)
    return (y * g.astype(jnp.float32)).astype(x.dtype)


def l2norm(x):
    return x * lax.rsqrt(jnp.sum(x * x, axis=-1, keepdims=True) + EPS)


def ada_modulation(c, w_ada, b_ada):
    mod = jnp.dot(jax.nn.silu(c), w_ada) + b_ada
    return jnp.split(mod[:, None, :], 6, axis=-1)


def causal_conv(u, buf, w):
    t = u.shape[1]
    full = jnp.concatenate([buf.astype(u.dtype), u], axis=1)
    y = full[:, 0:t] * w[0]
    for j in range(1, CONV_W):
        y = y + full[:, j:j + t] * w[j]
    return jax.nn.silu(y), full[:, t:]


def gated_delta_rule(q, k, v, g, beta, s0, chunk):
    b, t, h, dk = q.shape
    dv = v.shape[-1]
    n = t // chunk

    def blk(a):
        return a.reshape(b, n, chunk, h, *a.shape[3:]).swapaxes(2, 3)

    q, k, v, g, beta = (blk(a) for a in (q, k, v, g, beta))
    G = jnp.cumsum(g, axis=-1)
    causal = jnp.tril(jnp.ones((chunk, chunk), dtype=bool))
    strict = jnp.tril(jnp.ones((chunk, chunk), dtype=bool), k=-1)
    decay = jnp.exp(jnp.where(causal, G[..., :, None] - G[..., None, :], -jnp.inf))
    a_mat = jnp.where(strict, beta[..., :, None] * jnp.einsum('bnhid,bnhjd->bnhij', k, k) * decay, 0.0)
    rhs = jnp.concatenate([beta[..., None] * v, (beta * jnp.exp(G))[..., None] * k], axis=-1)
    sol = lax.linalg.triangular_solve(a_mat + jnp.eye(chunk, dtype=a_mat.dtype), rhs,
                                      left_side=True, lower=True, unit_diagonal=True)
    u, w = sol[..., :dv], sol[..., dv:]
    qk = jnp.einsum('bnhid,bnhjd->bnhij', q, k) * decay
    q_dec = q * jnp.exp(G)[..., None]
    k_dec = k * jnp.exp(G[..., -1:] - G)[..., None]
    g_last = jnp.exp(G[..., -1])

    def step(s, xs):
        u_c, w_c, qk_c, qd_c, kd_c, gl_c = xs
        e = u_c - jnp.einsum('bhck,bhkv->bhcv', w_c, s)
        o = jnp.einsum('bhck,bhkv->bhcv', qd_c, s) + jnp.einsum('bhij,bhjv->bhiv', qk_c, e)
        s = s * gl_c[..., None, None] + jnp.einsum('bhck,bhcv->bhkv', kd_c, e)
        return s, o

    xs = tuple(a.swapaxes(0, 1) for a in (u, w, qk, q_dec, k_dec, g_last))
    s_fin, o = lax.scan(step, s0, xs)
    o = o.transpose(1, 0, 3, 2, 4).reshape(b, t, h, dv)
    return o, s_fin


def deltanet_branch(u_qkv, z, beta_raw, a_raw, conv_buf, s0, conv_w, a_log, dt_bias, norm_g):
    b, t = u_qkv.shape[:2]
    f32 = jnp.float32
    y, new_buf = causal_conv(u_qkv, conv_buf, conv_w)
    qd, kd, vd = jnp.split(y.astype(f32), [DN_HEADS * DN_DK, 2 * DN_HEADS * DN_DK], axis=-1)
    q = l2norm(qd.reshape(b, t, DN_HEADS, DN_DK)) * (DN_DK ** -0.5)
    k = l2norm(kd.reshape(b, t, DN_HEADS, DN_DK))
    v = vd.reshape(b, t, DN_HEADS, DN_DV)
    beta = jax.nn.sigmoid(beta_raw.astype(f32))
    g = -jnp.exp(a_log.astype(f32)) * jax.nn.softplus(a_raw.astype(f32) + dt_bias.astype(f32))
    chunk = DN_CHUNK if t % DN_CHUNK == 0 else t
    o, s_new = gated_delta_rule(q, k, v, g, beta, s0.astype(f32), chunk)
    o = rms_norm(o, norm_g) * jax.nn.silu(z.astype(f32).reshape(b, t, DN_HEADS, DN_DV))
    return o.reshape(b, t, DN_WIDTH).astype(u_qkv.dtype), new_buf, s_new.astype(s0.dtype)


def indexer_scores(qi, wi, ki):
    s = jnp.einsum('bthd,bsd->bths', qi, ki, preferred_element_type=jnp.float32)
    return jnp.einsum('bths,bth->bts', jax.nn.relu(s), wi.astype(jnp.float32))


def attend_selected(q, k_sel, v_sel, valid):
    s = jnp.einsum('btngd,btknd->btngk', q, k_sel, preferred_element_type=jnp.float32) * (HEAD_DIM ** -0.5)
    s = jnp.where(valid[:, :, None, None, :], s, -jnp.inf)
    p = jax.nn.softmax(s, axis=-1)
    return jnp.einsum('btngk,btknd->btngd', p.astype(v_sel.dtype), v_sel)


def dsa_prompt(q, k, v, qi, wi, ki):
    b, s_len = q.shape[:2]
    n_top = min(INDEX_TOPK, s_len // 4)
    nb = s_len // Q_BLOCK
    pos_k = jnp.arange(s_len)
    gather = jax.vmap(lambda rows, idx: rows[idx])

    def blocks(a):
        return a.reshape(b, nb, Q_BLOCK, *a.shape[2:]).swapaxes(0, 1)

    def one_block(xs):
        qb, qib, wib, t0 = xs
        pos_q = t0 + jnp.arange(Q_BLOCK)
        sc = indexer_scores(qib, wib, ki)
        sc = jnp.where((pos_k[None, :] <= pos_q[:, None])[None], sc, -jnp.inf)
        _, idx = lax.top_k(sc, n_top)
        valid = idx <= pos_q[None, :, None]
        return attend_selected(qb, gather(k, idx), gather(v, idx), valid)

    o = lax.map(one_block, (blocks(q), blocks(qi), blocks(wi), jnp.arange(nb) * Q_BLOCK))
    return o.swapaxes(0, 1).reshape(b, s_len, *q.shape[2:])


def dsa_sample(q, k, v, qi, wi, ki, pool_k, pool_v, pool_ki, page_table):
    bd, t = q.shape[:2]
    n_pages = page_table.shape[1]
    past = n_pages * PAGE_SIZE
    n_keys = past + t
    n_top = min(INDEX_TOPK, n_keys // 4)
    ki_past = pool_ki[page_table].reshape(bd, past, IDX_DIM).astype(ki.dtype)
    sc = indexer_scores(qi, wi, jnp.concatenate([ki_past, ki], axis=1))
    pos_q = past + jnp.arange(t)
    sc = jnp.where((jnp.arange(n_keys)[None, :] <= pos_q[:, None])[None], sc, -jnp.inf)
    _, idx = lax.top_k(sc, n_top)
    valid = idx <= pos_q[None, :, None]
    in_past = idx < past
    idx_past = jnp.minimum(idx, past - 1)
    page = jnp.take_along_axis(page_table, (idx_past // PAGE_SIZE).reshape(bd, -1), axis=1).reshape(idx.shape)
    row = page * PAGE_SIZE + idx_past % PAGE_SIZE
    idx_new = jnp.clip(idx - past, 0, t - 1)
    gather = jax.vmap(lambda rows, i: rows[i])

    def select(pool, new):
        flat = pool.reshape(-1, *pool.shape[2:]).astype(new.dtype)
        return jnp.where(in_past[..., None, None], flat[row], gather(new, idx_new))

    return attend_selected(q, select(pool_k, k), select(pool_v, v), valid)


def moe_ffn(h, w_router, router_bias, w_e_gate, w_e_up, w_e_down, w_s_gate, w_s_up, w_s_down):
    b, t, d = h.shape
    f32 = jnp.float32
    xt = h.reshape(-1, d)
    n = xt.shape[0]
    scores = jax.nn.sigmoid(jnp.dot(xt, w_router, preferred_element_type=f32))
    biased = scores + router_bias.astype(f32)
    grp_score = lax.top_k(biased.reshape(n, N_GROUPS, -1), 2)[0].sum(-1)
    _, top_grp = lax.top_k(grp_score, TOPK_GROUPS)
    grp_mask = jnp.zeros((n, N_GROUPS), dtype=bool).at[jnp.arange(n)[:, None], top_grp].set(True)
    masked = jnp.where(jnp.repeat(grp_mask, N_EXPERTS // N_GROUPS, axis=1), biased, -jnp.inf)
    _, top_e = lax.top_k(masked, TOP_K)
    wts = jnp.take_along_axis(scores, top_e, axis=1)
    wts = wts / jnp.sum(wts, axis=-1, keepdims=True) * ROUTED_SCALE
    flat_e = top_e.reshape(-1)
    flat_tok = jnp.repeat(jnp.arange(n, dtype=jnp.int32), TOP_K)
    order = jnp.argsort(flat_e)
    se = flat_e[order]
    counts = jnp.zeros((N_EXPERTS,), jnp.int32).at[flat_e].add(1)
    starts = jnp.cumsum(counts) - counts
    padded = (counts + MOE_BLOCK - 1) // MOE_BLOCK * MOE_BLOCK
    pad_ends = jnp.cumsum(padded)
    pad_starts = pad_ends - padded
    dest = pad_starts[se] + jnp.arange(n * TOP_K, dtype=jnp.int32) - starts[se]
    n_blocks = -(-(n * TOP_K + N_EXPERTS * (MOE_BLOCK - 1)) // MOE_BLOCK)
    rows = n_blocks * MOE_BLOCK
    tok_buf = jnp.full((rows,), n, jnp.int32).at[dest].set(flat_tok[order])
    w_buf = jnp.zeros((rows,), f32).at[dest].set(wts.reshape(-1)[order])
    blk_e = jnp.minimum(jnp.searchsorted(pad_ends, jnp.arange(n_blocks, dtype=jnp.int32) * MOE_BLOCK, side='right'), N_EXPERTS - 1)
    x_pad = jnp.concatenate([xt, jnp.zeros((1, d), xt.dtype)], axis=0)

    def body(acc, xs):
        tok, wt, e = xs
        xb = x_pad[tok]
        hb = jax.nn.silu(jnp.dot(xb, w_e_gate[e])) * jnp.dot(xb, w_e_up[e])
        yb = jnp.dot(hb, w_e_down[e]).astype(f32) * wt[:, None]
        return acc.at[tok].add(yb), None

    acc, _ = lax.scan(body, jnp.zeros((n + 1, d), f32),
                      (tok_buf.reshape(n_blocks, MOE_BLOCK), w_buf.reshape(n_blocks, MOE_BLOCK), blk_e))
    shared = jnp.dot(jax.nn.silu(jnp.dot(xt, w_s_gate)) * jnp.dot(xt, w_s_up), w_s_down)
    return (acc[:n] + shared.astype(f32)).astype(h.dtype).reshape(b, t, d)


def decoder_layer(x, c, attn_fn, conv_buf, s0, lw):
    b, t, _ = x.shape
    shift1, scale1, gate1, shift2, scale2, gate2 = ada_modulation(c, lw['w_ada'], lw['b_ada'])
    h = rms_norm(x, lw['norm1_g']) * (1 + scale1) + shift1
    offsets = [int(o) for o in np.cumsum(IN_WIDTHS)[:-1]]
    (q, k, v, qi, ki, wi, u_qkv, z, beta_raw, a_raw, gate_a, gate_b) = jnp.split(jnp.dot(h, lw['w_in']), offsets, axis=-1)
    q = rms_norm(q.reshape(b, t, KV_HEADS, GQA_GROUPS, HEAD_DIM), lw['q_norm_g'])
    k = rms_norm(k.reshape(b, t, KV_HEADS, HEAD_DIM), lw['k_norm_g'])
    v = v.reshape(b, t, KV_HEADS, HEAD_DIM)
    qi = qi.reshape(b, t, IDX_HEADS, IDX_DIM)
    ki = rms_norm(ki, lw['kidx_norm_g'])
    wi = wi * ((IDX_HEADS * IDX_DIM) ** -0.5)
    o_a = attn_fn(q, k, v, qi, wi, ki).reshape(b, t, ATTN_WIDTH)
    o_b, new_buf, s_new = deltanet_branch(u_qkv, z, beta_raw, a_raw, conv_buf, s0, lw['conv_w'],
                                          lw['dn_a_log'], lw['dn_dt_bias'], lw['dn_norm_g'])
    merged = jax.nn.sigmoid(gate_a) * jnp.dot(o_a, lw['w_br_a']) + jax.nn.sigmoid(gate_b) * jnp.dot(o_b, lw['w_br_b'])
    x = x + gate1 * jnp.dot(merged, lw['w_out'])
    h2 = rms_norm(x, lw['norm2_g']) * (1 + scale2) + shift2
    x = x + gate2 * moe_ffn(h2, lw['w_router'], lw['router_bias'], lw['w_e_gate'], lw['w_e_up'], lw['w_e_down'],
                            lw['w_s_gate'], lw['w_s_up'], lw['w_s_down'])
    return x, (k, v, ki, new_buf, s_new)


def setup_inputs(seed: int = 0) -> dict:
    key = jax.random.key(seed)
    ks = iter(jax.random.split(key, 48))
    f32 = jnp.float32

    def nrm(shape, scale=1.0):
        return jax.random.normal(next(ks), shape, f32) * scale

    def gain(n):
        return 1.0 + nrm((DEPTH, n), 0.1)

    n_pages = PAST_LEN // PAGE_SIZE
    n_used = DEC_BATCH * n_pages
    n_phys = n_used + max(1, n_used // 4)
    page_table = jax.random.permutation(next(ks), n_phys)[:n_used].reshape(DEC_BATCH, n_pages).astype(jnp.int32)
    dt = jnp.exp(jax.random.uniform(next(ks), (DEPTH, DN_HEADS), f32, math.log(1e-3), math.log(1e-1)))
    dn_dt_bias = dt + jnp.log(-jnp.expm1(-dt))
    dn_a_log = jnp.log(jax.random.uniform(next(ks), (DEPTH, DN_HEADS), f32, 1.0, 16.0))
    return {
        'x_prompt': nrm((BATCH, SEQ, D_MODEL)),
        'x_sample': nrm((DEC_BATCH, DEC_SEQ, D_MODEL)),
        'cache_k': nrm((DEPTH, n_phys, PAGE_SIZE, KV_HEADS, HEAD_DIM)),
        'cache_v': nrm((DEPTH, n_phys, PAGE_SIZE, KV_HEADS, HEAD_DIM)),
        'cache_kidx': nrm((DEPTH, n_phys, PAGE_SIZE, IDX_DIM)),
        'state_conv': nrm((DEPTH, DEC_BATCH, CONV_W - 1, DN_CONV_CH)),
        'state_delta': nrm((DEPTH, DEC_BATCH, DN_HEADS, DN_DK, DN_DV), 0.3),
        'page_table': page_table,
        'c_prompt': nrm((BATCH, D_MODEL)),
        'c_sample': nrm((DEC_BATCH, D_MODEL)),
        'w_ada': nrm((DEPTH, D_MODEL, 6 * D_MODEL), 0.5 * D_MODEL ** -0.5),
        'b_ada': nrm((DEPTH, 6 * D_MODEL), 0.02),
        'norm1_g': gain(D_MODEL),
        'w_in': nrm((DEPTH, D_MODEL, N_IN), D_MODEL ** -0.5),
        'q_norm_g': gain(HEAD_DIM),
        'k_norm_g': gain(HEAD_DIM),
        'kidx_norm_g': gain(IDX_DIM),
        'conv_w': nrm((DEPTH, CONV_W, DN_CONV_CH), CONV_W ** -0.5),
        'dn_a_log': dn_a_log,
        'dn_dt_bias': dn_dt_bias,
        'dn_norm_g': gain(DN_DV),
        'w_br_a': nrm((DEPTH, ATTN_WIDTH, D_MODEL), ATTN_WIDTH ** -0.5),
        'w_br_b': nrm((DEPTH, DN_WIDTH, D_MODEL), DN_WIDTH ** -0.5),
        'w_out': nrm((DEPTH, D_MODEL, D_MODEL), D_MODEL ** -0.5),
        'norm2_g': gain(D_MODEL),
        'w_router': nrm((DEPTH, D_MODEL, N_EXPERTS), D_MODEL ** -0.5),
        'router_bias': nrm((DEPTH, N_EXPERTS), 0.01),
        'w_e_gate': nrm((DEPTH, N_EXPERTS, D_MODEL, D_EXPERT), D_MODEL ** -0.5),
        'w_e_up': nrm((DEPTH, N_EXPERTS, D_MODEL, D_EXPERT), D_MODEL ** -0.5),
        'w_e_down': nrm((DEPTH, N_EXPERTS, D_EXPERT, D_MODEL), D_EXPERT ** -0.5),
        'w_s_gate': nrm((DEPTH, D_MODEL, D_SHARED), D_MODEL ** -0.5),
        'w_s_up': nrm((DEPTH, D_MODEL, D_SHARED), D_MODEL ** -0.5),
        'w_s_down': nrm((DEPTH, D_SHARED, D_MODEL), D_SHARED ** -0.5),
    }


def reference(x_prompt, x_sample, cache_k, cache_v, cache_kidx, state_conv, state_delta, page_table,
              c_prompt, c_sample, w_ada, b_ada, norm1_g, w_in, q_norm_g, k_norm_g, kidx_norm_g, conv_w,
              dn_a_log, dn_dt_bias, dn_norm_g, w_br_a, w_br_b, w_out, norm2_g, w_router, router_bias,
              w_e_gate, w_e_up, w_e_down, w_s_gate, w_s_up, w_s_down):
    y_prompt, y_sample = x_prompt, x_sample
    bp = x_prompt.shape[0]
    kp, vp, kip, cp, sp = [], [], [], [], []
    kq, vq, kiq, cq, sq = [], [], [], [], []
    for l in range(DEPTH):
        lw = dict(w_ada=w_ada[l], b_ada=b_ada[l], norm1_g=norm1_g[l], w_in=w_in[l], q_norm_g=q_norm_g[l],
                  k_norm_g=k_norm_g[l], kidx_norm_g=kidx_norm_g[l], conv_w=conv_w[l], dn_a_log=dn_a_log[l],
                  dn_dt_bias=dn_dt_bias[l], dn_norm_g=dn_norm_g[l], w_br_a=w_br_a[l], w_br_b=w_br_b[l],
                  w_out=w_out[l], norm2_g=norm2_g[l], w_router=w_router[l], router_bias=router_bias[l],
                  w_e_gate=w_e_gate[l], w_e_up=w_e_up[l], w_e_down=w_e_down[l], w_s_gate=w_s_gate[l],
                  w_s_up=w_s_up[l], w_s_down=w_s_down[l])
        zero_buf = jnp.zeros((bp, CONV_W - 1, DN_CONV_CH), x_prompt.dtype)
        zero_s = jnp.zeros((bp, DN_HEADS, DN_DK, DN_DV), state_delta.dtype)
        y_prompt, st_p = decoder_layer(y_prompt, c_prompt, dsa_prompt, zero_buf, zero_s, lw)
        attn_s = functools.partial(dsa_sample, pool_k=cache_k[l], pool_v=cache_v[l], pool_ki=cache_kidx[l],
                                   page_table=page_table)
        y_sample, st_s = decoder_layer(y_sample, c_sample, attn_s, state_conv[l], state_delta[l], lw)
        for lst, a in zip((kp, vp, kip, cp, sp), st_p):
            lst.append(a)
        for lst, a in zip((kq, vq, kiq, cq, sq), st_s):
            lst.append(a)
    return (y_prompt, y_sample,
            jnp.stack(kp), jnp.stack(vp), jnp.stack(kip), jnp.stack(cp), jnp.stack(sp),
            jnp.stack(kq), jnp.stack(vq), jnp.stack(kiq), jnp.stack(cq), jnp.stack(sq))
```

```python
import functools
import math

import jax
import jax.numpy as jnp
from jax import lax
from jax.experimental import pallas as pl
from jax.experimental.pallas import tpu as pltpu

F32, BF16, I32 = jnp.float32, jnp.bfloat16, jnp.int32
HIGHEST = lax.Precision.HIGHEST

CFG = dict(
    kv_heads=4, idx_heads=32, index_topk=256, page_size=128,
    dn_chunk=64, conv_w=4, n_groups=8, topk_groups=4, top_k=8,
    routed_scale=2.5, eps=1e-6,
)

LANES = 128
SUBLANES = 8
VMEM_LIMIT = 56 * 1024 * 1024
NEG_BIG = -1e30
INT_MIN = -(2 ** 31)


def _params(*sem):
    return pltpu.CompilerParams(dimension_semantics=sem, vmem_limit_bytes=VMEM_LIMIT)


def _pick(n, cands):
    for c in cands:
        if n % c == 0:
            return c
    raise ValueError(f"no tile in {cands} divides {n}")


def _sigmoid(x):
    return 1.0 / (1.0 + jnp.exp(-x))


def _silu(x):
    return x * _sigmoid(x)


def _mm_body(*refs, n_a, n_e, nk, prologue, epilogue):
    a_refs = refs[:n_a]
    b_ref = refs[n_a]
    e_refs = refs[n_a + 1:n_a + 1 + n_e]
    o_ref = refs[n_a + 1 + n_e]
    j = pl.program_id(1)
    a = prologue(*[r[...] for r in a_refs]) if prologue else a_refs[0][...]
    part = jnp.dot(a.astype(BF16), b_ref[...].astype(BF16), preferred_element_type=F32)

    def finish(acc):
        res = epilogue(acc, j, *[e[...] for e in e_refs]) if epilogue else acc
        o_ref[...] = res.astype(o_ref.dtype)

    if nk == 1:
        finish(part)
    else:
        acc_ref = refs[n_a + 2 + n_e]
        k = pl.program_id(2)

        @pl.when(k == 0)
        def _():
            acc_ref[...] = part

        @pl.when(k > 0)
        def _():
            acc_ref[...] += part

        @pl.when(k == nk - 1)
        def _():
            finish(acc_ref[...])


def _mm(a_list, b, *, tm, tn, tk=None, prologue=None, epilogue=None, extras=(),
        out_dtype=F32, name="mm"):
    m, kdim = a_list[0].shape
    n = b.shape[1]
    tk = kdim if tk is None else tk
    nk = kdim // tk
    assert m % tm == 0 and n % tn == 0 and kdim % tk == 0, (m, n, kdim, tm, tn, tk)
    in_specs = [pl.BlockSpec((tm, tk), lambda i, j, k: (i, k)) for _ in a_list]
    in_specs.append(pl.BlockSpec((tk, tn), lambda i, j, k: (k, j)))
    ex_arrays = []
    for e in extras:
        if isinstance(e, tuple):
            arr, bs, im = e
            in_specs.append(pl.BlockSpec(bs, im))
        elif e.shape[0] == 1:
            arr = e
            in_specs.append(pl.BlockSpec((1, tn), lambda i, j, k: (0, j)))
        else:
            arr = e
            in_specs.append(pl.BlockSpec((tm, tn), lambda i, j, k: (i, j)))
        ex_arrays.append(arr)
    body = functools.partial(_mm_body, n_a=len(a_list), n_e=len(ex_arrays), nk=nk,
                             prologue=prologue, epilogue=epilogue)
    scratch = [pltpu.VMEM((tm, tn), F32)] if nk > 1 else []
    return pl.pallas_call(
        body,
        grid=(m // tm, n // tn, nk),
        in_specs=in_specs,
        out_specs=pl.BlockSpec((tm, tn), lambda i, j, k: (i, j)),
        out_shape=jax.ShapeDtypeStruct((m, n), out_dtype),
        scratch_shapes=scratch,
        compiler_params=_params("parallel", "parallel", "arbitrary"),
        name=name,
    )(*a_list, b, *ex_arrays)


def _normmod_body(x_ref, g_ref, sc_ref, sh_ref, o_ref, *, eps):
    x = x_ref[...]
    ms = jnp.mean(x * x, axis=-1, keepdims=True)
    y = x * lax.rsqrt(ms + eps) * g_ref[...]
    o_ref[...] = (y * (1.0 + sc_ref[...]) + sh_ref[...]).astype(o_ref.dtype)


def _normmod(x, g, scale2, shift2, *, tmod, n_prompt_tiles, out_dtype):
    m, d = x.shape
    mod_map = lambda i: (jnp.where(i >= n_prompt_tiles, 1, 0), 0)
    return pl.pallas_call(
        functools.partial(_normmod_body, eps=CFG["eps"]),
        grid=(m // tmod,),
        in_specs=[pl.BlockSpec((tmod, d), lambda i: (i, 0)),
                  pl.BlockSpec((1, d), lambda i: (0, 0)),
                  pl.BlockSpec((tmod, d), mod_map),
                  pl.BlockSpec((tmod, d), mod_map)],
        out_specs=pl.BlockSpec((tmod, d), lambda i: (i, 0)),
        out_shape=jax.ShapeDtypeStruct((m, d), out_dtype),
        compiler_params=_params("parallel"),
        name="normmod",
    )(x, g, scale2, shift2)


def _ep_bias(acc, j, bias):
    return acc + bias


def _group_rmsnorm(x, g, width, eps, scale):
    outs = []
    for c in range(x.shape[1] // width):
        blk = x[:, c * width:(c + 1) * width]
        ms = jnp.mean(blk * blk, axis=-1, keepdims=True)
        outs.append(blk * lax.rsqrt(ms + eps) * g[:, c * width:(c + 1) * width] * scale)
    return outs[0] if len(outs) == 1 else jnp.concatenate(outs, axis=1)


def _ep_qnorm(acc, j, g, *, hd, eps, scale):
    return _group_rmsnorm(acc, g, hd, eps, scale)


def _ep_kvnorm(acc, j, g, *, hd, eps, n_k_tiles):
    normed = _group_rmsnorm(acc, g, hd, eps, 1.0)
    return jnp.where(j < n_k_tiles, normed, acc)


def _ep_kw(acc, j, g, *, idx_dim, eps, wscale):
    lane = lax.broadcasted_iota(I32, acc.shape, 1)
    is_k = lane < idx_dim
    ms = jnp.sum(jnp.where(is_k, acc * acc, 0.0), axis=-1, keepdims=True) * (1.0 / idx_dim)
    return jnp.where(is_k, acc * lax.rsqrt(ms + eps) * g, acc * wscale)


def _ep_sigmoid(acc, j):
    return _sigmoid(acc)


def _sortable_keys(x):
    bits = lax.bitcast_convert_type(x, I32)
    return jnp.where(bits < 0, bits ^ 0x7FFFFFFF, bits)


def _head_scores(q2, w2, kc, rows, ih):
    s = lax.dot_general(q2, kc, (((1,), (1,)), ((), ())), preferred_element_type=F32)
    r = jnp.maximum(s, 0.0) * w2
    return jnp.sum(r.reshape(rows, ih, kc.shape[0]), axis=1)


def _kth_threshold(key_ref, nch, tk, ktop, rows):
    def count_ge(cand):
        def body(c, acc):
            off = pl.multiple_of(c * tk, tk)
            kk = key_ref[:, pl.ds(off, tk)]
            for s in range(tk // LANES):
                acc = acc + jnp.where(kk[:, s * LANES:(s + 1) * LANES] >= cand, 1, 0)
            return acc
        acc = lax.fori_loop(0, nch, body, jnp.zeros((rows, LANES), I32))
        return jnp.sum(acc, axis=-1, keepdims=True)

    def bit_step(b, t):
        cand = t + lax.shift_left(jnp.int32(1), 31 - b)
        return jnp.where(count_ge(cand) >= ktop, cand, t)

    return lax.fori_loop(0, 32, bit_step, jnp.full((rows, 1), INT_MIN, I32))


def _write_bias(key_ref, bias_store, thr, nch, nch_total, tk, rows, dtype):
    def sel_chunk(c, _):
        off = pl.multiple_of(c * tk, tk)
        kk = key_ref[:, pl.ds(off, tk)]
        sel = (kk >= thr) & (kk > INT_MIN)
        bias_store(off, jnp.where(sel, 0.0, NEG_BIG).astype(dtype))
        return 0

    def fill_chunk(c, _):
        off = pl.multiple_of(c * tk, tk)
        bias_store(off, jnp.full((rows, tk), NEG_BIG, dtype))
        return 0

    lax.fori_loop(0, nch, sel_chunk, 0)
    lax.fori_loop(nch, nch_total, fill_chunk, 0)


def _idx_prompt_body(q2_ref, w2_ref, kin_ref, bias_ref, key_ref, *, tq, tk, ih, ktop, s_len):
    t0 = pl.program_id(0) * tq
    nch = (t0 + tq + tk - 1) // tk
    q2 = q2_ref[...]
    w2 = w2_ref[...]
    row = t0 + lax.broadcasted_iota(I32, (tq, tk), 0)
    col = lax.broadcasted_iota(I32, (tq, tk), 1)

    def score_chunk(c, _):
        off = pl.multiple_of(c * tk, tk)
        sc = _head_scores(q2, w2, kin_ref[pl.ds(off, tk), :], tq, ih)
        key_ref[:, pl.ds(off, tk)] = jnp.where(col + off <= row, _sortable_keys(sc), INT_MIN)
        return 0

    lax.fori_loop(0, nch, score_chunk, 0)
    thr = _kth_threshold(key_ref, nch, tk, ktop, tq)

    def store(off, val):
        bias_ref[:, pl.ds(off, tk)] = val

    _write_bias(key_ref, store, thr, nch, s_len // tk, tk, tq, bias_ref.dtype)


def _idx_prompt(q2, w2, kin, *, s_len, ih, ktop, tq, tk):
    idim = q2.shape[1]
    body = functools.partial(_idx_prompt_body, tq=tq, tk=tk, ih=ih, ktop=ktop, s_len=s_len)
    return pl.pallas_call(
        body,
        grid=(s_len // tq,),
        in_specs=[pl.BlockSpec((tq * ih, idim), lambda i: (i, 0)),
                  pl.BlockSpec((tq * ih, 1), lambda i: (i, 0)),
                  pl.BlockSpec((s_len, idim), lambda i: (0, 0))],
        out_specs=pl.BlockSpec((tq, s_len), lambda i: (i, 0)),
        out_shape=jax.ShapeDtypeStruct((s_len, s_len), BF16),
        scratch_shapes=[pltpu.VMEM((tq, s_len), I32)],
        compiler_params=_params("parallel"),
        name="idx_prompt",
    )(q2, w2, kin)


def _idx_sample_body(pt_ref, q2_ref, w2_ref, page_ref, knew_ref, bias_ref, key_ref,
                     *, t, ih, ktop, n_pages, psz):
    p = pl.program_id(1)
    kb = jnp.where(p < n_pages, page_ref[0], knew_ref[0]).astype(BF16)
    sc = _head_scores(q2_ref[0], w2_ref[0], kb, t, ih)
    row = lax.broadcasted_iota(I32, (t, psz), 0)
    col = lax.broadcasted_iota(I32, (t, psz), 1)
    valid = jnp.logical_or(p < n_pages, col <= row)
    key_ref[:, pl.ds(pl.multiple_of(p * psz, psz), psz)] = jnp.where(valid, _sortable_keys(sc), INT_MIN)

    @pl.when(p == n_pages)
    def _():
        thr = _kth_threshold(key_ref, n_pages + 1, psz, ktop, t)

        def store(off, val):
            bias_ref[0, :, pl.ds(off, psz)] = val

        _write_bias(key_ref, store, thr, n_pages + 1, n_pages + 1, psz, t, bias_ref.dtype)


def _idx_sample(page_table, q2, w2, pool_ki, knew, *, t, ih, ktop):
    bd, n_pages = page_table.shape
    psz, idim = pool_ki.shape[1], pool_ki.shape[2]
    width = (n_pages + 1) * psz
    body = functools.partial(_idx_sample_body, t=t, ih=ih, ktop=ktop, n_pages=n_pages, psz=psz)
    grid_spec = pltpu.PrefetchScalarGridSpec(
        num_scalar_prefetch=1,
        grid=(bd, n_pages + 1),
        in_specs=[pl.BlockSpec((1, t * ih, idim), lambda b, p, pt: (b, 0, 0)),
                  pl.BlockSpec((1, t * ih, 1), lambda b, p, pt: (b, 0, 0)),
                  pl.BlockSpec((1, psz, idim),
                               lambda b, p, pt: (pt[b, jnp.minimum(p, n_pages - 1)], 0, 0)),
                  pl.BlockSpec((1, psz, idim), lambda b, p, pt: (b, 0, 0))],
        out_specs=pl.BlockSpec((1, t, width), lambda b, p, pt: (b, 0, 0)),
        scratch_shapes=[pltpu.VMEM((t, width), I32)],
    )
    return pl.pallas_call(
        body, grid_spec=grid_spec,
        out_shape=jax.ShapeDtypeStruct((bd, t, width), F32),
        compiler_params=_params("parallel", "arbitrary"),
        name="idx_sample",
    )(page_table, q2, w2, pool_ki, knew)


def _softmax_step(s, v, m_prev, l_prev, acc_prev):
    m_new = jnp.maximum(m_prev, jnp.max(s, axis=-1, keepdims=True))
    alpha = jnp.exp(m_prev - m_new)
    p = jnp.exp(s - m_new)
    l_new = alpha * l_prev + jnp.sum(p, axis=-1, keepdims=True)
    acc_new = alpha * acc_prev + jnp.dot(p.astype(BF16), v, preferred_element_type=F32)
    return m_new, l_new, acc_new


def _attn_prompt_body(q_ref, k_ref, v_ref, b_ref, o_ref, m_ref, l_ref, acc_ref, *, nh, kvh, hd):
    i = pl.program_id(0)
    j = pl.program_id(1)
    grp = nh // kvh

    @pl.when(j == 0)
    def _():
        m_ref[...] = jnp.full(m_ref.shape, -jnp.inf, F32)
        l_ref[...] = jnp.zeros(l_ref.shape, F32)
        acc_ref[...] = jnp.zeros(acc_ref.shape, F32)

    @pl.when(j <= i)
    def _():
        bias = b_ref[...].astype(F32)
        for n in range(kvh):
            kn = k_ref[:, n * hd:(n + 1) * hd].astype(BF16)
            vn = v_ref[:, n * hd:(n + 1) * hd].astype(BF16)
            for g in range(grp):
                h = n * grp + g
                hs = slice(h * hd, (h + 1) * hd)
                s = lax.dot_general(q_ref[:, hs], kn, (((1,), (1,)), ((), ())),
                                    preferred_element_type=F32) + bias
                m_ref[h], l_ref[h], acc_ref[:, hs] = _softmax_step(
                    s, vn, m_ref[h], l_ref[h], acc_ref[:, hs])

    @pl.when(j == i)
    def _():
        for h in range(nh):
            hs = slice(h * hd, (h + 1) * hd)
            o_ref[:, hs] = (acc_ref[:, hs] / l_ref[h]).astype(o_ref.dtype)


def _attn_prompt(q, kv, bias, *, s_len, nh, kvh, hd, tq):
    aw, kvw = nh * hd, kvh * hd
    nt = s_len // tq
    body = functools.partial(_attn_prompt_body, nh=nh, kvh=kvh, hd=hd)
    return pl.pallas_call(
        body,
        grid=(nt, nt),
        in_specs=[pl.BlockSpec((tq, aw), lambda i, j: (i, 0)),
                  pl.BlockSpec((tq, kvw), lambda i, j: (jnp.minimum(i, j), 0)),
                  pl.BlockSpec((tq, kvw), lambda i, j: (jnp.minimum(i, j), 1)),
                  pl.BlockSpec((tq, tq), lambda i, j: (i, jnp.minimum(i, j)))],
        out_specs=pl.BlockSpec((tq, aw), lambda i, j: (i, 0)),
        out_shape=jax.ShapeDtypeStruct((s_len, aw), BF16),
        scratch_shapes=[pltpu.VMEM((nh, tq, 1), F32), pltpu.VMEM((nh, tq, 1), F32),
                        pltpu.VMEM((tq, aw), F32)],
        compiler_params=_params("parallel", "arbitrary"),
        name="attn_prompt",
    )(q, kv, kv, bias)


def _attn_sample_body(pt_ref, q_ref, kp_ref, vp_ref, kn_ref, vn_ref, b_ref, o_ref,
                      m_ref, l_ref, acc_ref, *, t, nh, kvh, hd, n_pages):
    p = pl.program_id(1)
    grp = nh // kvh

    @pl.when(p == 0)
    def _():
        m_ref[...] = jnp.full(m_ref.shape, -jnp.inf, F32)
        l_ref[...] = jnp.zeros(l_ref.shape, F32)
        acc_ref[...] = jnp.zeros(acc_ref.shape, F32)

    past = p < n_pages
    kblk = jnp.where(past, kp_ref[0], kn_ref[0])
    vblk = jnp.where(past, vp_ref[0], vn_ref[0])
    bias = jnp.concatenate([b_ref[0]] * grp, axis=0)
    q = q_ref[0]
    for n in range(kvh):
        qn = jnp.concatenate([q[:, (n * grp + g) * hd:(n * grp + g + 1) * hd] for g in range(grp)],
                             axis=0).astype(BF16)
        kn = kblk[:, n * hd:(n + 1) * hd].astype(BF16)
        vn = vblk[:, n * hd:(n + 1) * hd].astype(BF16)
        s = lax.dot_general(qn, kn, (((1,), (1,)), ((), ())), preferred_element_type=F32) + bias
        m_ref[n], l_ref[n], acc_ref[n] = _softmax_step(s, vn, m_ref[n], l_ref[n], acc_ref[n])

    @pl.when(p == n_pages)
    def _():
        for n in range(kvh):
            on = acc_ref[n] / l_ref[n]
            for g in range(grp):
                h = n * grp + g
                o_ref[0, :, h * hd:(h + 1) * hd] = on[g * t:(g + 1) * t].astype(o_ref.dtype)


def _attn_sample(page_table, q, pool_k, pool_v, knew, vnew, bias, *, t, nh, kvh, hd):
    bd, n_pages = page_table.shape
    psz = pool_k.shape[1]
    aw, kvw = nh * hd, kvh * hd
    grp = nh // kvh
    body = functools.partial(_attn_sample_body, t=t, nh=nh, kvh=kvh, hd=hd, n_pages=n_pages)
    page_map = lambda b, p, pt: (pt[b, jnp.minimum(p, n_pages - 1)], 0, 0)
    own_map = lambda b, p, pt: (b, 0, 0)
    grid_spec = pltpu.PrefetchScalarGridSpec(
        num_scalar_prefetch=1,
        grid=(bd, n_pages + 1),
        in_specs=[pl.BlockSpec((1, t, aw), own_map),
                  pl.BlockSpec((1, psz, kvw), page_map),
                  pl.BlockSpec((1, psz, kvw), page_map),
                  pl.BlockSpec((1, psz, kvw), own_map),
                  pl.BlockSpec((1, psz, kvw), own_map),
                  pl.BlockSpec((1, t, psz), lambda b, p, pt: (b, 0, p))],
        out_specs=pl.BlockSpec((1, t, aw), own_map),
        scratch_shapes=[pltpu.VMEM((kvh, grp * t, 1), F32), pltpu.VMEM((kvh, grp * t, 1), F32),
                        pltpu.VMEM((kvh, grp * t, hd), F32)],
    )
    return pl.pallas_call(
        body, grid_spec=grid_spec,
        out_shape=jax.ShapeDtypeStruct((bd, t, aw), F32),
        compiler_params=_params("parallel", "arbitrary"),
        name="attn_sample",
    )(page_table, q, pool_k, pool_v, knew, vnew, bias)


def _softplus(x):
    return jnp.maximum(x, 0.0) + jnp.log(1.0 + jnp.exp(-jnp.abs(x)))


def _dn_prep_body(u_ref, prev_ref, hist_ref, cw_ref, ba_ref, al_ref, dtb_ref, y_ref, bg_ref,
                  *, tm, tiles_per_seq, nh, hd, conv_w, eps):
    first = (pl.program_id(0) % tiles_per_seq) == 0
    r8 = lax.broadcasted_iota(I32, (SUBLANES, hd), 0)
    qscale = hd ** -0.5
    for c in range(u_ref.shape[1] // hd):
        cs = slice(c * hd, (c + 1) * hd)
        cur = u_ref[:, cs]
        prev = jnp.where(first, hist_ref[0, :, cs], prev_ref[:, cs])
        y = cur * cw_ref[conv_w - 1:conv_w, cs]
        for k in range(1, conv_w):
            sk = pltpu.roll(cur, k, 0)
            top = jnp.where(r8 < k, pltpu.roll(prev, k, 0), sk[:SUBLANES])
            sk = top if tm == SUBLANES else jnp.concatenate([top, sk[SUBLANES:]], axis=0)
            y = y + sk * cw_ref[conv_w - 1 - k:conv_w - k, cs]
        y = _silu(y)
        if c < 2 * nh:
            y = y * lax.rsqrt(jnp.sum(y * y, axis=-1, keepdims=True) + eps)
            if c < nh:
                y = y * qscale
        y_ref[:, cs] = y
    ba = ba_ref[...]
    lane = lax.broadcasted_iota(I32, ba.shape, 1)
    decay = -jnp.exp(al_ref[...]) * _softplus(ba + dtb_ref[...])
    bg_ref[...] = jnp.where(lane < nh, _sigmoid(ba), jnp.where(lane < 2 * nh, decay, 0.0))


def _dn_prep(u, hist8, cw8, ba, al_row, dtb_row, *, row_off, nrows, tm, seq_len, nh, hd):
    cch = u.shape[1]
    tps = seq_len // tm
    body = functools.partial(_dn_prep_body, tm=tm, tiles_per_seq=tps, nh=nh, hd=hd,
                             conv_w=CFG["conv_w"], eps=CFG["eps"])
    ob, o8 = row_off // tm, row_off // SUBLANES
    return pl.pallas_call(
        body,
        grid=(nrows // tm,),
        in_specs=[pl.BlockSpec((tm, cch), lambda i: (ob + i, 0)),
                  pl.BlockSpec((SUBLANES, cch),
                               lambda i: (jnp.maximum(o8 + i * (tm // SUBLANES) - 1, 0), 0)),
                  pl.BlockSpec((1, SUBLANES, cch), lambda i: (i // tps, 0, 0)),
                  pl.BlockSpec((SUBLANES, cch), lambda i: (0, 0)),
                  pl.BlockSpec((tm, LANES), lambda i: (ob + i, 0)),
                  pl.BlockSpec((1, LANES), lambda i: (0, 0)),
                  pl.BlockSpec((1, LANES), lambda i: (0, 0))],
        out_specs=[pl.BlockSpec((tm, cch), lambda i: (i, 0)),
                   pl.BlockSpec((tm, LANES), lambda i: (i, 0))],
        out_shape=[jax.ShapeDtypeStruct((nrows, cch), F32),
                   jax.ShapeDtypeStruct((nrows, LANES), F32)],
        compiler_params=_params("parallel"),
        name="dn_prep",
    )(u, u, hist8, cw8, ba, al_row, dtb_row)


def _dot_hi(a, b):
    return jnp.dot(a, b, precision=HIGHEST, preferred_element_type=F32)


def _dot_nt_hi(a, b):
    return lax.dot_general(a, b, (((1,), (1,)), ((), ())), precision=HIGHEST,
                           preferred_element_type=F32)


def _dot_tn_hi(a, b):
    return lax.dot_general(a, b, (((0,), (0,)), ((), ())), precision=HIGHEST,
                           preferred_element_type=F32)


def _dn_scan_body(y_ref, bg_ref, z_ref, s0_ref, ng_ref, o_ref, s_ref, *, c, nh, hd, eps):
    @pl.when(pl.program_id(1) == 0)
    def _():
        s_ref[...] = s0_ref[...]

    bg = bg_ref[...]
    ri = lax.broadcasted_iota(I32, (c, c), 0)
    ci = lax.broadcasted_iota(I32, (c, c), 1)
    causal = ri >= ci
    strict = ri > ci
    eye = jnp.where(ri == ci, 1.0, 0.0)
    gcum = _dot_hi(jnp.where(causal, 1.0, 0.0), bg)
    gpad = gcum if c == LANES else jnp.concatenate([gcum, jnp.zeros((LANES - c, LANES), F32)], axis=0)
    g_t = gpad.T
    n_double = int(math.log2(c)) - 1
    for h in range(nh):
        q = y_ref[:, h * hd:(h + 1) * hd]
        k = y_ref[:, (nh + h) * hd:(nh + h + 1) * hd]
        v = y_ref[:, (2 * nh + h) * hd:(2 * nh + h + 1) * hd]
        beta = bg[:, h:h + 1]
        gcol = gcum[:, nh + h:nh + h + 1]
        grow = g_t[nh + h:nh + h + 1, :c]
        glast = gcum[c - 1:c, nh + h:nh + h + 1]
        decay = jnp.where(causal, jnp.exp(jnp.where(causal, gcol - grow, 0.0)), 0.0)
        a = jnp.where(strict, beta * _dot_nt_hi(k, k) * decay, 0.0)
        p = -a
        tinv = eye + p
        for _ in range(n_double):
            p = _dot_hi(p, p)
            tinv = tinv + _dot_hi(tinv, p)
        eg = jnp.exp(gcol)
        u = _dot_hi(tinv, beta * v)
        w = _dot_hi(tinv, (beta * eg) * k)
        qk = _dot_nt_hi(q, k) * decay
        s = s_ref[0, h]
        e = u - _dot_hi(w, s)
        o = _dot_hi(q * eg, s) + _dot_hi(qk, e)
        s_ref[0, h] = s * jnp.exp(glast) + _dot_tn_hi(k * jnp.exp(glast - gcol), e)
        on = o * lax.rsqrt(jnp.mean(o * o, axis=-1, keepdims=True) + eps) * ng_ref[...]
        o_ref[:, h * hd:(h + 1) * hd] = (on * _silu(z_ref[:, h * hd:(h + 1) * hd])).astype(o_ref.dtype)


def _dn_scan(y, bg, z, s0, ng_row, *, z_row_off, c, seq_len, nh, hd):
    nseq = s0.shape[0]
    nch = seq_len // c
    zo = z_row_off // c
    body = functools.partial(_dn_scan_body, c=c, nh=nh, hd=hd, eps=CFG["eps"])
    return pl.pallas_call(
        body,
        grid=(nseq, nch),
        in_specs=[pl.BlockSpec((c, y.shape[1]), lambda b, n: (b * nch + n, 0)),
                  pl.BlockSpec((c, LANES), lambda b, n: (b * nch + n, 0)),
                  pl.BlockSpec((c, nh * hd), lambda b, n: (zo + b * nch + n, 0)),
                  pl.BlockSpec((1, nh, hd, hd), lambda b, n: (b, 0, 0, 0)),
                  pl.BlockSpec((1, hd), lambda b, n: (0, 0))],
        out_specs=[pl.BlockSpec((c, nh * hd), lambda b, n: (b * nch + n, 0)),
                   pl.BlockSpec((1, nh, hd, hd), lambda b, n: (b, 0, 0, 0))],
        out_shape=[jax.ShapeDtypeStruct((nseq * seq_len, nh * hd), F32),
                   jax.ShapeDtypeStruct(s0.shape, F32)],
        compiler_params=_params("parallel", "arbitrary"),
        name="dn_scan",
    )(y, bg, z, s0, ng_row)


def _first_argmax(x, lane, width):
    m = jnp.max(x, axis=-1, keepdims=True)
    return jnp.min(jnp.where(x == m, lane, width), axis=-1, keepdims=True)


def _router_body(h_ref, wr_ref, rb_ref, te_ref, tr_ref, tw_ref, cnt_ref,
                 *, tm, ne, n_groups, topk_groups, top_k, scale):
    @pl.when(pl.program_id(0) == 0)
    def _():
        cnt_ref[...] = jnp.zeros(cnt_ref.shape, F32)

    logits = jnp.dot(h_ref[...].astype(BF16), wr_ref[...], preferred_element_type=F32)
    scores = _sigmoid(logits)
    biased = scores + rb_ref[...]
    lane = lax.broadcasted_iota(I32, (tm, ne), 1)
    lane_grp = lax.shift_right_logical(lane, int(math.log2(ne // n_groups)))
    ninf = -jnp.inf
    gscore = jnp.full((tm, ne), ninf, F32)
    for g in range(n_groups):
        xg = jnp.where(lane_grp == g, biased, ninf)
        m1 = jnp.max(xg, axis=-1, keepdims=True)
        i1 = jnp.min(jnp.where(xg == m1, lane, ne), axis=-1, keepdims=True)
        m2 = jnp.max(jnp.where(lane == i1, ninf, xg), axis=-1, keepdims=True)
        gscore = jnp.where(lane == g, m1 + m2, gscore)
    gsel = jnp.zeros((tm, ne), F32)
    for _ in range(topk_groups):
        ig = _first_argmax(gscore, lane, ne)
        gsel = jnp.where(lane_grp == ig, 1.0, gsel)
        gscore = jnp.where(lane == ig, ninf, gscore)
    x = jnp.where(gsel > 0.0, biased, ninf)
    sel = jnp.zeros((tm, ne), F32)
    idxs, ws = [], []
    for _ in range(top_k):
        ie = _first_argmax(x, lane, ne)
        hit = lane == ie
        idxs.append(ie)
        ws.append(jnp.sum(jnp.where(hit, scores, 0.0), axis=-1, keepdims=True))
        sel = jnp.where(hit, 1.0, sel)
        x = jnp.where(hit, ninf, x)
    wsum = functools.reduce(lambda p, q: p + q, ws)
    r = lax.broadcasted_iota(I32, (tm, tm), 0)
    cc = lax.broadcasted_iota(I32, (tm, tm), 1)
    earlier = jnp.where(r > cc, 1.0, 0.0).astype(BF16)
    rank = cnt_ref[...] + jnp.dot(earlier, sel.astype(BF16), preferred_element_type=F32)
    cnt_ref[...] += jnp.sum(sel, axis=0, keepdims=True)
    te = jnp.zeros((tm, ne), I32)
    tr = jnp.zeros((tm, ne), F32)
    tw = jnp.zeros((tm, ne), F32)
    for kk in range(top_k):
        slot = lane == kk
        te = jnp.where(slot, idxs[kk], te)
        tr = jnp.where(slot, jnp.sum(jnp.where(lane == idxs[kk], rank, 0.0), axis=-1, keepdims=True), tr)
        tw = jnp.where(slot, ws[kk] / wsum * scale, tw)
    te_ref[...] = te
    tr_ref[...] = tr
    tw_ref[...] = tw


def _router(h2, w_router, rb_row, *, tm):
    m, d = h2.shape
    ne = w_router.shape[1]
    assert ne == LANES
    body = functools.partial(_router_body, tm=tm, ne=ne, n_groups=CFG["n_groups"],
                             topk_groups=CFG["topk_groups"], top_k=CFG["top_k"],
                             scale=CFG["routed_scale"])
    tile = pl.BlockSpec((tm, ne), lambda i: (i, 0))
    return pl.pallas_call(
        body,
        grid=(m // tm,),
        in_specs=[pl.BlockSpec((tm, d), lambda i: (i, 0)),
                  pl.BlockSpec((d, ne), lambda i: (0, 0)),
                  pl.BlockSpec((1, ne), lambda i: (0, 0))],
        out_specs=[tile, tile, tile, pl.BlockSpec((1, ne), lambda i: (0, 0))],
        out_shape=[jax.ShapeDtypeStruct((m, ne), I32), jax.ShapeDtypeStruct((m, ne), F32),
                   jax.ShapeDtypeStruct((m, ne), F32), jax.ShapeDtypeStruct((1, ne), F32)],
        compiler_params=_params("arbitrary"),
        name="router",
    )(h2, w_router, rb_row)


def _dest_body(te_ref, tr_ref, ps_ref, d_ref, *, top_k):
    te = te_ref[...]
    tr = tr_ref[...]
    lane = lax.broadcasted_iota(I32, te.shape, 1)
    d = jnp.zeros(te.shape, F32)
    for kk in range(top_k):
        base = jnp.sum(jnp.where(lane == te[:, kk:kk + 1], ps_ref[...], 0.0), axis=-1, keepdims=True)
        d = jnp.where(lane == kk, base + tr[:, kk:kk + 1], d)
    d_ref[...] = d.astype(I32)


def _dest(te, tr, pad_start_row, *, tm):
    m, ne = te.shape
    tile = pl.BlockSpec((tm, ne), lambda i: (i, 0))
    return pl.pallas_call(
        functools.partial(_dest_body, top_k=CFG["top_k"]),
        grid=(m // tm,),
        in_specs=[tile, tile, pl.BlockSpec((1, ne), lambda i: (0, 0))],
        out_specs=tile,
        out_shape=jax.ShapeDtypeStruct((m, ne), I32),
        compiler_params=_params("parallel"),
        name="dest",
    )(te, tr, pad_start_row)


def _scatter_body(dest_ref, x_hbm, xs_in, xs_hbm, sem, *, chunk, top_k, window):
    del xs_in
    base_tok = pl.program_id(0) * (chunk // top_k)

    def copy(a):
        tok = base_tok + a // top_k
        return pltpu.make_async_copy(x_hbm.at[pl.ds(tok, 1)], xs_hbm.at[pl.ds(dest_ref[a], 1)], sem)

    def issue(a, _):
        copy(a).start()

        @pl.when(a >= window)
        def _():
            copy(a - window).wait()
        return 0

    def drain(a, _):
        copy(a).wait()
        return 0

    lax.fori_loop(0, chunk, issue, 0)
    lax.fori_loop(chunk - window, chunk, drain, 0)


def _scatter_rows(dest_flat, x, xs_zero, *, chunk):
    top_k = CFG["top_k"]
    n_assign = dest_flat.shape[0]
    body = functools.partial(_scatter_body, chunk=chunk, top_k=top_k, window=min(32, chunk))
    return pl.pallas_call(
        body,
        grid=(n_assign // chunk,),
        in_specs=[pl.BlockSpec((chunk,), lambda i: (i,), memory_space=pltpu.SMEM),
                  pl.BlockSpec(memory_space=pl.ANY),
                  pl.BlockSpec(memory_space=pl.ANY)],
        out_specs=pl.BlockSpec(memory_space=pl.ANY),
        out_shape=jax.ShapeDtypeStruct(xs_zero.shape, xs_zero.dtype),
        scratch_shapes=[pltpu.SemaphoreType.DMA(())],
        input_output_aliases={2: 0},
        compiler_params=_params("arbitrary"),
        name="moe_scatter",
    )(dest_flat, x, xs_zero)


def _experts_body(sbe_ref, nu_ref, xs_ref, wg_ref, wu_ref, wd_ref, y_ref):
    c = pl.program_id(1)

    @pl.when(pl.program_id(0) < nu_ref[0])
    def _():
        xb = xs_ref[...].astype(BF16)
        g = jnp.dot(xb, wg_ref[0].astype(BF16), preferred_element_type=F32)
        u = jnp.dot(xb, wu_ref[0].astype(BF16), preferred_element_type=F32)
        hmid = (_silu(g) * u).astype(BF16)
        y = jnp.dot(hmid, wd_ref[0].astype(BF16), preferred_element_type=F32)

        @pl.when(c == 0)
        def _():
            y_ref[...] = y

        @pl.when(c > 0)
        def _():
            y_ref[...] += y

    @pl.when(jnp.logical_and(pl.program_id(0) >= nu_ref[0], c == 0))
    def _():
        y_ref[...] = jnp.zeros(y_ref.shape, F32)


def _experts(sb_e, n_used, xs, w_gate, w_up, w_down, *, sb, ce):
    rows, d = xs.shape
    de = w_gate.shape[2]
    nc = de // ce

    def s_eff(s, nu):
        return jnp.minimum(s, nu[0] - 1)

    def c_eff(s, c, nu):
        return jnp.where(s < nu[0], c, nc - 1)

    grid_spec = pltpu.PrefetchScalarGridSpec(
        num_scalar_prefetch=2,
        grid=(rows // sb, nc),
        in_specs=[pl.BlockSpec((sb, d), lambda s, c, sbe, nu: (s_eff(s, nu), 0)),
                  pl.BlockSpec((1, d, ce), lambda s, c, sbe, nu: (sbe[s_eff(s, nu)], 0, c_eff(s, c, nu))),
                  pl.BlockSpec((1, d, ce), lambda s, c, sbe, nu: (sbe[s_eff(s, nu)], 0, c_eff(s, c, nu))),
                  pl.BlockSpec((1, ce, d), lambda s, c, sbe, nu: (sbe[s_eff(s, nu)], c_eff(s, c, nu), 0))],
        out_specs=pl.BlockSpec((sb, d), lambda s, c, sbe, nu: (s, 0)),
    )
    return pl.pallas_call(
        _experts_body, grid_spec=grid_spec,
        out_shape=jax.ShapeDtypeStruct((rows, d), F32),
        compiler_params=_params("arbitrary", "arbitrary"),
        name="moe_experts",
    )(sb_e, n_used, xs, w_gate, w_up, w_down)


def _combine_body(dest_ref, ys_hbm, w_ref, sh_ref, x1_ref, g2_ref, o_ref, buf, sem, *, tc, top_k):
    def copy(a):
        return pltpu.make_async_copy(ys_hbm.at[pl.ds(dest_ref[a], 1)],
                                     buf.at[a // top_k, pl.ds(a % top_k, 1)], sem)

    def issue(a, _):
        copy(a).start()
        return 0

    def drain(a, _):
        copy(a).wait()
        return 0

    lax.fori_loop(0, tc * top_k, issue, 0)
    lax.fori_loop(0, tc * top_k, drain, 0)
    routed = jnp.sum(buf[...] * w_ref[...], axis=1)
    o_ref[...] = x1_ref[...] + g2_ref[...] * (routed + sh_ref[...])


def _combine(dest_flat, ys, w3, shared, x1, gate2, *, tc, tmod, n_prompt_tiles):
    top_k = CFG["top_k"]
    m, d = x1.shape
    per = tmod // tc
    npt = n_prompt_tiles * per

    def g2_map(i):
        return (jnp.where(i >= npt, per, 0) + i % per, 0)

    row = pl.BlockSpec((tc, d), lambda i: (i, 0))
    return pl.pallas_call(
        functools.partial(_combine_body, tc=tc, top_k=top_k),
        grid=(m // tc,),
        in_specs=[pl.BlockSpec((tc * top_k,), lambda i: (i,), memory_space=pltpu.SMEM),
                  pl.BlockSpec(memory_space=pl.ANY),
                  pl.BlockSpec((tc, top_k, 1), lambda i: (i, 0, 0)),
                  row, row, pl.BlockSpec((tc, d), g2_map)],
        out_specs=row,
        out_shape=jax.ShapeDtypeStruct((m, d), F32),
        scratch_shapes=[pltpu.VMEM((tc, top_k, d), F32), pltpu.SemaphoreType.DMA(())],
        compiler_params=_params("arbitrary"),
        name="moe_combine",
    )(dest_flat, ys, w3, shared, x1, gate2)


SUPER_BLOCK = 256
EXPERT_CHUNK = 256
ROW_TILES = (768, 512, 256, 128, 64, 32, 16)
COL_TILES = (512, 256, 128)


def _pad_rows(a, n, axis, front=False):
    pad = [(0, 0)] * a.ndim
    pad[axis] = (n - a.shape[axis], 0) if front else (0, n - a.shape[axis])
    return jnp.pad(a, pad)


def _lane_row(vec, offset):
    return jnp.zeros((1, LANES), F32).at[0, offset:offset + vec.shape[0]].set(vec.astype(F32))


def _moe(h2, x1, gate2, lw, *, tmod, npt, tmm):
    m, d = h2.shape
    top_k = CFG["top_k"]
    ne = lw["w_router"].shape[1]
    te, tr, tw, counts = _router(h2, lw["w_router"].astype(BF16), lw["router_bias"][None].astype(F32),
                                 tm=tmod)
    sb = SUPER_BLOCK
    n_sb = -(-(m * top_k + ne * (sb - 1)) // sb)
    cnt = counts[0].astype(I32)
    padded = (cnt + sb - 1) // sb * sb
    pad_end = jnp.cumsum(padded)
    pad_start = pad_end - padded
    n_used = (pad_end[-1:] // sb).astype(I32)
    sb_e = jnp.minimum(jnp.searchsorted(pad_end, jnp.arange(n_sb, dtype=I32) * sb, side="right"),
                       ne - 1).astype(I32)
    dest = _dest(te, tr, pad_start.astype(F32)[None], tm=tmod)[:, :top_k].reshape(-1)
    xs = _scatter_rows(dest, h2, jnp.zeros((n_sb * sb, d), F32), chunk=tmod * top_k)
    ys = _experts(sb_e, n_used, xs, lw["w_e_gate"], lw["w_e_up"], lw["w_e_down"],
                  sb=sb, ce=min(EXPERT_CHUNK, lw["w_e_gate"].shape[2]))
    tns = _pick(lw["w_s_gate"].shape[1], COL_TILES)
    gs = _mm([h2], lw["w_s_gate"].astype(BF16), tm=tmm, tn=tns, name="shared_gate")
    us = _mm([h2], lw["w_s_up"].astype(BF16), tm=tmm, tn=tns, name="shared_up")
    shared = _mm([gs, us], lw["w_s_down"].astype(BF16), tm=tmm, tn=_pick(d, COL_TILES),
                 prologue=lambda g, u: _silu(g) * u, name="shared_down")
    return _combine(dest, ys, tw[:, :top_k].reshape(m, top_k, 1), shared, x1, gate2,
                    tc=min(32, tmod), tmod=tmod, n_prompt_tiles=npt)


def _layer(xp, xs, cp, cs, pool_k, pool_v, pool_ki, st_conv, st_delta, page_table, lw):
    eps = CFG["eps"]
    bp, s_len, d = xp.shape
    bd, t, _ = xs.shape
    assert bp == 1 and t % SUBLANES == 0 and t >= CFG["conv_w"] - 1
    n_s = bd * t
    m = s_len + n_s
    tmod = n_s
    assert s_len % tmod == 0
    npt = s_len // tmod
    tmm = _pick(m, ROW_TILES)
    hd = lw["q_norm_g"].shape[0]
    aw = lw["w_br_a"].shape[0]
    nh = aw // hd
    n_phys, psz, kvh, _ = pool_k.shape
    kvw = kvh * hd
    idim = pool_ki.shape[2]
    dnh = lw["dn_a_log"].shape[0]
    dv = lw["dn_norm_g"].shape[0]
    dnw = lw["w_br_b"].shape[0]
    cch = lw["conv_w"].shape[1]
    assert cch == 3 * dnw and dnw == dnh * dv
    n_in = lw["w_in"].shape[1]
    ih = (n_in - (aw + 2 * kvw + idim + cch + dnw + 2 * dnh + 2 * d)) // (idim + 1)
    assert idim + ih <= LANES and 2 * dnh <= LANES

    n_c = bp + bd
    c_all = _pad_rows(jnp.concatenate([cp, cs], axis=0), -(-n_c // 16) * 16, 0)
    mod = _mm([c_all], lw["w_ada"], tm=c_all.shape[0], tn=_pick(6 * d, COL_TILES),
              prologue=_silu, epilogue=_ep_bias, extras=[lw["b_ada"][None]], name="ada")

    def expand(a):
        return jnp.concatenate([jnp.broadcast_to(a[0:1], (tmod, d)),
                                jnp.repeat(a[bp:bp + bd], t, axis=0)], axis=0)

    shift1, scale1, gate1, shift2, scale2, gate2 = (expand(a) for a in jnp.split(mod, 6, axis=1))
    x_all = jnp.concatenate([xp.reshape(s_len, d), xs.reshape(n_s, d)], axis=0)
    h = _normmod(x_all, lw["norm1_g"][None], scale1, shift1, tmod=tmod, n_prompt_tiles=npt,
                 out_dtype=BF16)

    offs = [0]
    for wdt in (aw, kvw, kvw, ih * idim, idim, ih, cch, dnw, dnh, dnh, d, d):
        offs.append(offs[-1] + wdt)
    w_in = lw["w_in"]

    def cols(a, b, pad_to=None):
        w = w_in[:, offs[a]:offs[b]].astype(BF16)
        return w if pad_to is None else _pad_rows(w, pad_to, 1)

    qn = _mm([h], cols(0, 1), tm=tmm, tn=_pick(aw, COL_TILES), out_dtype=BF16, name="proj_q",
             epilogue=functools.partial(_ep_qnorm, hd=hd, eps=eps, scale=hd ** -0.5),
             extras=[jnp.tile(lw["q_norm_g"], nh)[None]])
    tkv = _pick(kvw, COL_TILES)
    kv = _mm([h], cols(1, 3), tm=tmm, tn=tkv, name="proj_kv",
             epilogue=functools.partial(_ep_kvnorm, hd=hd, eps=eps, n_k_tiles=kvw // tkv),
             extras=[jnp.concatenate([jnp.tile(lw["k_norm_g"], kvh), jnp.ones((kvw,), F32)])[None]])
    qi = _mm([h], cols(3, 4), tm=tmm, tn=_pick(ih * idim, COL_TILES), out_dtype=BF16, name="proj_qi")
    kw = _mm([h], cols(4, 6, LANES), tm=tmm, tn=LANES, name="proj_kw",
             epilogue=functools.partial(_ep_kw, idx_dim=idim, eps=eps, wscale=(ih * idim) ** -0.5),
             extras=[_lane_row(lw["kidx_norm_g"], 0)])
    u = _mm([h], cols(6, 7), tm=tmm, tn=_pick(cch, COL_TILES), name="proj_u")
    z = _mm([h], cols(7, 8), tm=tmm, tn=_pick(dnw, COL_TILES), name="proj_z")
    ba = _mm([h], cols(8, 10, LANES), tm=tmm, tn=LANES, name="proj_ba")
    gates = _mm([h], cols(10, 12), tm=tmm, tn=_pick(d, COL_TILES), epilogue=_ep_sigmoid, name="proj_g")

    ki_all = kw[:, :idim]
    wi_all = kw[:, idim:idim + ih]
    bias_p = _idx_prompt(qi[:s_len].reshape(s_len * ih, idim), wi_all[:s_len].reshape(s_len * ih, 1),
                         ki_all[:s_len].astype(BF16), s_len=s_len, ih=ih,
                         ktop=min(CFG["index_topk"], s_len // 4),
                         tq=_pick(s_len, (128, 64, 32, 16)), tk=_pick(s_len, (256, 128)))
    oa_p = _attn_prompt(qn, kv, bias_p, s_len=s_len, nh=nh, kvh=kvh, hd=hd, tq=_pick(s_len, (256, 128)))
    n_pages = page_table.shape[1]
    bias_s = _idx_sample(page_table, qi[s_len:].reshape(bd, t * ih, idim),
                         wi_all[s_len:].reshape(bd, t * ih, 1), pool_ki,
                         _pad_rows(ki_all[s_len:].reshape(bd, t, idim), psz, 1),
                         t=t, ih=ih, ktop=min(CFG["index_topk"], (n_pages * psz + t) // 4))
    kv_s = kv[s_len:].reshape(bd, t, 2 * kvw)
    oa_s = _attn_sample(page_table, qn[s_len:].astype(F32).reshape(bd, t, aw),
                        pool_k.reshape(n_phys, psz, kvw), pool_v.reshape(n_phys, psz, kvw),
                        _pad_rows(kv_s[:, :, :kvw], psz, 1), _pad_rows(kv_s[:, :, kvw:], psz, 1),
                        bias_s, t=t, nh=nh, kvh=kvh, hd=hd)
    o_a = jnp.concatenate([oa_p, oa_s.reshape(n_s, aw).astype(BF16)], axis=0)

    cw8 = _pad_rows(lw["conv_w"], SUBLANES, 0)
    al_row = _lane_row(lw["dn_a_log"], dnh)
    dtb_row = _lane_row(lw["dn_dt_bias"], dnh)
    ng_row = lw["dn_norm_g"][None]
    y_p, bg_p = _dn_prep(u, jnp.zeros((bp, SUBLANES, cch), F32), cw8, ba, al_row, dtb_row,
                         row_off=0, nrows=s_len, tm=_pick(s_len, (128, 64, 32, 16, 8)),
                         seq_len=s_len, nh=dnh, hd=dv)
    y_s, bg_s = _dn_prep(u, _pad_rows(st_conv.astype(F32), SUBLANES, 1, front=True), cw8, ba, al_row,
                         dtb_row, row_off=s_len, nrows=n_s, tm=SUBLANES, seq_len=t, nh=dnh, hd=dv)
    chunk = CFG["dn_chunk"]
    ob_p, sd_p = _dn_scan(y_p, bg_p, z, jnp.zeros((bp,) + st_delta.shape[1:], F32), ng_row,
                          z_row_off=0, c=chunk if s_len % chunk == 0 else s_len, seq_len=s_len,
                          nh=dnh, hd=dv)
    ob_s, sd_s = _dn_scan(y_s, bg_s, z, st_delta.astype(F32), ng_row,
                          z_row_off=s_len, c=chunk if t % chunk == 0 else t, seq_len=t, nh=dnh, hd=dv)
    o_b = jnp.concatenate([ob_p, ob_s], axis=0)

    tnd = _pick(d, COL_TILES)
    ma = _mm([o_a], lw["w_br_a"].astype(BF16), tm=tmm, tn=tnd, name="branch_a",
             epilogue=lambda acc, j, g: acc * g,
             extras=[(gates, (tmm, tnd), lambda i, j, k: (i, j))])
    merged = _mm([o_b], lw["w_br_b"].astype(BF16), tm=tmm, tn=tnd, out_dtype=BF16, name="branch_b",
                 epilogue=lambda acc, j, g, other: acc * g + other,
                 extras=[(gates, (tmm, tnd), lambda i, j, k: (i, j + d // tnd)), ma])
    x1 = _mm([merged], lw["w_out"].astype(BF16), tm=tmod, tn=tnd, name="out_proj",
             epilogue=lambda acc, j, x, g1: x + g1 * acc,
             extras=[x_all, (gate1, (tmod, tnd), lambda i, j, k: (jnp.where(i >= npt, 1, 0), j))])

    h2 = _normmod(x1, lw["norm2_g"][None], scale2, shift2, tmod=tmod, n_prompt_tiles=npt, out_dtype=F32)
    y = _moe(h2, x1, gate2, lw, tmod=tmod, npt=npt, tmm=tmm)

    nb = CFG["conv_w"] - 1
    u_s = u[s_len:].reshape(bd, t, cch)
    state_p = (kv[:s_len, :kvw].reshape(bp, s_len, kvh, hd), kv[:s_len, kvw:].reshape(bp, s_len, kvh, hd),
               ki_all[:s_len].reshape(bp, s_len, idim), u[s_len - nb:s_len][None], sd_p)
    state_s = (kv_s[:, :, :kvw].reshape(bd, t, kvh, hd), kv_s[:, :, kvw:].reshape(bd, t, kvh, hd),
               ki_all[s_len:].reshape(bd, t, idim), u_s[:, t - nb:], sd_s)
    return y[:s_len].reshape(bp, s_len, d), y[s_len:].reshape(bd, t, d), state_p, state_s


_LAYER_WEIGHTS = ("w_ada", "b_ada", "norm1_g", "w_in", "q_norm_g", "k_norm_g", "kidx_norm_g", "conv_w",
                  "dn_a_log", "dn_dt_bias", "dn_norm_g", "w_br_a", "w_br_b", "w_out", "norm2_g",
                  "w_router", "router_bias", "w_e_gate", "w_e_up", "w_e_down", "w_s_gate", "w_s_up",
                  "w_s_down")


def kernel(x_prompt, x_sample, cache_k, cache_v, cache_kidx, state_conv, state_delta, page_table,
           c_prompt, c_sample, w_ada, b_ada, norm1_g, w_in, q_norm_g, k_norm_g, kidx_norm_g, conv_w,
           dn_a_log, dn_dt_bias, dn_norm_g, w_br_a, w_br_b, w_out, norm2_g, w_router, router_bias,
           w_e_gate, w_e_up, w_e_down, w_s_gate, w_s_up, w_s_down):
    weights = dict(zip(_LAYER_WEIGHTS, (w_ada, b_ada, norm1_g, w_in, q_norm_g, k_norm_g, kidx_norm_g,
                                        conv_w, dn_a_log, dn_dt_bias, dn_norm_g, w_br_a, w_br_b, w_out,
                                        norm2_g, w_router, router_bias, w_e_gate, w_e_up, w_e_down,
                                        w_s_gate, w_s_up, w_s_down)))
    y_p, y_s = x_prompt, x_sample
    states_p, states_s = [], []
    for layer in range(w_ada.shape[0]):
        lw = {name: w[layer] for name, w in weights.items()}
        y_p, y_s, st_p, st_s = _layer(y_p, y_s, c_prompt, c_sample, cache_k[layer], cache_v[layer],
                                      cache_kidx[layer], state_conv[layer], state_delta[layer],
                                      page_table, lw)
        states_p.append(st_p)
        states_s.append(st_s)
    stack = lambda sts, i: jnp.stack([st[i] for st in sts])
    return (y_p, y_s) + tuple(stack(states_p, i) for i in range(5)) + tuple(stack(states_s, i) for i in range(5))
```

```python
import functools
import math

import jax
import jax.numpy as jnp
from jax import lax
from jax.experimental import pallas as pl
from jax.experimental.pallas import tpu as pltpu

F32, BF16, I32 = jnp.float32, jnp.bfloat16, jnp.int32
HIGHEST = lax.Precision.HIGHEST

CFG = dict(
    kv_heads=4, idx_heads=32, index_topk=256, page_size=128,
    dn_chunk=64, conv_w=4, n_groups=8, topk_groups=4, top_k=8,
    routed_scale=2.5, eps=1e-6,
)

LANES = 128
SUBLANES = 8
VMEM_LIMIT = 56 * 1024 * 1024
NEG_BIG = -1e30
INT_MIN = -(2 ** 31)


def _params(*sem):
    return pltpu.CompilerParams(dimension_semantics=sem, vmem_limit_bytes=VMEM_LIMIT)


def _pick(n, cands):
    for c in cands:
        if n % c == 0:
            return c
    raise ValueError(f"no tile in {cands} divides {n}")


def _sigmoid(x):
    return 1.0 / (1.0 + jnp.exp(-x))


def _silu(x):
    return x * _sigmoid(x)


def _mm_body(*refs, n_a, n_e, nk, prologue, epilogue):
    a_refs = refs[:n_a]
    b_ref = refs[n_a]
    e_refs = refs[n_a + 1:n_a + 1 + n_e]
    o_ref = refs[n_a + 1 + n_e]
    j = pl.program_id(1)
    a = prologue(*[r[...] for r in a_refs]) if prologue else a_refs[0][...]
    part = jnp.dot(a.astype(BF16), b_ref[...].astype(BF16), preferred_element_type=F32)

    def finish(acc):
        res = epilogue(acc, j, *[e[...] for e in e_refs]) if epilogue else acc
        o_ref[...] = res.astype(o_ref.dtype)

    if nk == 1:
        finish(part)
    else:
        acc_ref = refs[n_a + 2 + n_e]
        k = pl.program_id(2)

        @pl.when(k == 0)
        def _():
            acc_ref[...] = part

        @pl.when(k > 0)
        def _():
            acc_ref[...] += part

        @pl.when(k == nk - 1)
        def _():
            finish(acc_ref[...])


def _mm(a_list, b, *, tm, tn, tk=None, prologue=None, epilogue=None, extras=(),
        out_dtype=F32, name="mm"):
    m, kdim = a_list[0].shape
    n = b.shape[1]
    tk = kdim if tk is None else tk
    nk = kdim // tk
    assert m % tm == 0 and n % tn == 0 and kdim % tk == 0, (m, n, kdim, tm, tn, tk)
    in_specs = [pl.BlockSpec((tm, tk), lambda i, j, k: (i, k)) for _ in a_list]
    in_specs.append(pl.BlockSpec((tk, tn), lambda i, j, k: (k, j)))
    ex_arrays = []
    for e in extras:
        if isinstance(e, tuple):
            arr, bs, im = e
            in_specs.append(pl.BlockSpec(bs, im))
        elif e.shape[0] == 1:
            arr = e
            in_specs.append(pl.BlockSpec((1, tn), lambda i, j, k: (0, j)))
        else:
            arr = e
            in_specs.append(pl.BlockSpec((tm, tn), lambda i, j, k: (i, j)))
        ex_arrays.append(arr)
    body = functools.partial(_mm_body, n_a=len(a_list), n_e=len(ex_arrays), nk=nk,
                             prologue=prologue, epilogue=epilogue)
    scratch = [pltpu.VMEM((tm, tn), F32)] if nk > 1 else []
    return pl.pallas_call(
        body,
        grid=(m // tm, n // tn, nk),
        in_specs=in_specs,
        out_specs=pl.BlockSpec((tm, tn), lambda i, j, k: (i, j)),
        out_shape=jax.ShapeDtypeStruct((m, n), out_dtype),
        scratch_shapes=scratch,
        compiler_params=_params("parallel", "parallel", "arbitrary"),
        name=name,
    )(*a_list, b, *ex_arrays)


def _normmod_body(x_ref, g_ref, sc_ref, sh_ref, o_ref, *, eps):
    x = x_ref[...]
    ms = jnp.mean(x * x, axis=-1, keepdims=True)
    y = x * lax.rsqrt(ms + eps) * g_ref[...]
    o_ref[...] = (y * (1.0 + sc_ref[...]) + sh_ref[...]).astype(o_ref.dtype)


def _normmod(x, g, scale2, shift2, *, tmod, n_prompt_tiles, out_dtype):
    m, d = x.shape
    mod_map = lambda i: (jnp.where(i >= n_prompt_tiles, 1, 0), 0)
    return pl.pallas_call(
        functools.partial(_normmod_body, eps=CFG["eps"]),
        grid=(m // tmod,),
        in_specs=[pl.BlockSpec((tmod, d), lambda i: (i, 0)),
                  pl.BlockSpec((1, d), lambda i: (0, 0)),
                  pl.BlockSpec((tmod, d), mod_map),
                  pl.BlockSpec((tmod, d), mod_map)],
        out_specs=pl.BlockSpec((tmod, d), lambda i: (i, 0)),
        out_shape=jax.ShapeDtypeStruct((m, d), out_dtype),
        compiler_params=_params("parallel"),
        name="normmod",
    )(x, g, scale2, shift2)


def _ep_bias(acc, j, bias):
    return acc + bias


def _group_rmsnorm(x, g, width, eps, scale):
    outs = []
    for c in range(x.shape[1] // width):
        blk = x[:, c * width:(c + 1) * width]
        ms = jnp.mean(blk * blk, axis=-1, keepdims=True)
        outs.append(blk * lax.rsqrt(ms + eps) * g[:, c * width:(c + 1) * width] * scale)
    return outs[0] if len(outs) == 1 else jnp.concatenate(outs, axis=1)


def _ep_qnorm(acc, j, g, *, hd, eps, scale):
    return _group_rmsnorm(acc, g, hd, eps, scale)


def _ep_kvnorm(acc, j, g, *, hd, eps, n_k_tiles):
    normed = _group_rmsnorm(acc, g, hd, eps, 1.0)
    return jnp.where(j < n_k_tiles, normed, acc)


def _ep_kw(acc, j, g, *, idx_dim, eps, wscale):
    lane = lax.broadcasted_iota(I32, acc.shape, 1)
    is_k = lane < idx_dim
    ms = jnp.sum(jnp.where(is_k, acc * acc, 0.0), axis=-1, keepdims=True) * (1.0 / idx_dim)
    return jnp.where(is_k, acc * lax.rsqrt(ms + eps) * g, acc * wscale)


def _ep_sigmoid(acc, j):
    return _sigmoid(acc)


def _sortable_keys(x):
    bits = lax.bitcast_convert_type(x, I32)
    return jnp.where(bits < 0, bits ^ 0x7FFFFFFF, bits)


def _head_scores(q2, w2, kc, rows, ih):
    s = lax.dot_general(q2, kc, (((1,), (1,)), ((), ())), preferred_element_type=F32)
    r = jnp.maximum(s, 0.0) * w2
    return jnp.sum(r.reshape(rows, ih, kc.shape[0]), axis=1)


def _kth_threshold(key_ref, nch, tk, ktop, rows):
    def count_ge(cand):
        def body(c, acc):
            off = pl.multiple_of(c * tk, tk)
            kk = key_ref[:, pl.ds(off, tk)]
            for s in range(tk // LANES):
                acc = acc + jnp.where(kk[:, s * LANES:(s + 1) * LANES] >= cand, 1, 0)
            return acc
        acc = lax.fori_loop(0, nch, body, jnp.zeros((rows, LANES), I32))
        return jnp.sum(acc, axis=-1, keepdims=True)

    def bit_step(b, t):
        cand = t + lax.shift_left(jnp.int32(1), 31 - b)
        return jnp.where(count_ge(cand) >= ktop, cand, t)

    return lax.fori_loop(0, 32, bit_step, jnp.full((rows, 1), INT_MIN, I32))


def _write_bias(key_ref, bias_store, thr, ktop, nch, nch_total, tk, rows, dtype):
    def count_gt(c, acc):
        kk = key_ref[:, pl.ds(pl.multiple_of(c * tk, tk), tk)]
        return acc + jnp.sum(jnp.where(kk > thr, 1.0, 0.0), axis=-1, keepdims=True)

    need = ktop - lax.fori_loop(0, nch, count_gt, jnp.zeros((rows, 1), F32))
    upper = (lax.broadcasted_iota(I32, (tk, tk), 0) <= lax.broadcasted_iota(I32, (tk, tk), 1))
    upper = jnp.where(upper, 1.0, 0.0).astype(BF16)

    def sel_chunk(c, seen):
        off = pl.multiple_of(c * tk, tk)
        kk = key_ref[:, pl.ds(off, tk)]
        tie = jnp.logical_and(kk == thr, kk > INT_MIN)
        tie_f = jnp.where(tie, 1.0, 0.0)
        prefix = seen + jnp.dot(tie_f.astype(BF16), upper, preferred_element_type=F32)
        sel = jnp.logical_or(kk > thr, jnp.logical_and(tie, prefix <= need))
        bias_store(off, jnp.where(sel, 0.0, NEG_BIG).astype(dtype))
        return seen + jnp.sum(tie_f, axis=-1, keepdims=True)

    def fill_chunk(c, _):
        off = pl.multiple_of(c * tk, tk)
        bias_store(off, jnp.full((rows, tk), NEG_BIG, dtype))
        return 0

    lax.fori_loop(0, nch, sel_chunk, jnp.zeros((rows, 1), F32))
    lax.fori_loop(nch, nch_total, fill_chunk, 0)


def _idx_prompt_body(q2_ref, w2_ref, kin_ref, bias_ref, key_ref, *, tq, tk, ih, ktop, s_len):
    t0 = pl.program_id(0) * tq
    nch = (t0 + tq + tk - 1) // tk
    q2 = q2_ref[...]
    w2 = w2_ref[...]
    row = t0 + lax.broadcasted_iota(I32, (tq, tk), 0)
    col = lax.broadcasted_iota(I32, (tq, tk), 1)

    def score_chunk(c, _):
        off = pl.multiple_of(c * tk, tk)
        sc = _head_scores(q2, w2, kin_ref[pl.ds(off, tk), :], tq, ih)
        key_ref[:, pl.ds(off, tk)] = jnp.where(col + off <= row, _sortable_keys(sc), INT_MIN)
        return 0

    lax.fori_loop(0, nch, score_chunk, 0)
    thr = _kth_threshold(key_ref, nch, tk, ktop, tq)

    def store(off, val):
        bias_ref[:, pl.ds(off, tk)] = val

    _write_bias(key_ref, store, thr, ktop, nch, s_len // tk, tk, tq, bias_ref.dtype)


def _idx_prompt(q2, w2, kin, *, s_len, ih, ktop, tq, tk):
    idim = q2.shape[1]
    body = functools.partial(_idx_prompt_body, tq=tq, tk=tk, ih=ih, ktop=ktop, s_len=s_len)
    return pl.pallas_call(
        body,
        grid=(s_len // tq,),
        in_specs=[pl.BlockSpec((tq * ih, idim), lambda i: (i, 0)),
                  pl.BlockSpec((tq * ih, 1), lambda i: (i, 0)),
                  pl.BlockSpec((s_len, idim), lambda i: (0, 0))],
        out_specs=pl.BlockSpec((tq, s_len), lambda i: (i, 0)),
        out_shape=jax.ShapeDtypeStruct((s_len, s_len), BF16),
        scratch_shapes=[pltpu.VMEM((tq, s_len), I32)],
        compiler_params=_params("parallel"),
        name="idx_prompt",
    )(q2, w2, kin)


PAGES_PER_STEP = 8


def _page_steps(n_pages):
    return -(-(n_pages + 1) // PAGES_PER_STEP)


def _page_specs(block, n_pages):
    def spec(r):
        return pl.BlockSpec(block, lambda b, p, pt: (pt[b, jnp.minimum(p * PAGES_PER_STEP + r, n_pages - 1)],)
                            + (0,) * (len(block) - 1))
    return [spec(r) for r in range(PAGES_PER_STEP)]


def _idx_sample_body(pt_ref, q2_ref, w2_ref, *rest, t, ih, ktop, n_pages, psz, n_steps):
    page_refs = rest[:PAGES_PER_STEP]
    knew_ref, bias_ref, key_ref = rest[PAGES_PER_STEP:]
    p = pl.program_id(1)
    q2 = q2_ref[0]
    w2 = w2_ref[0]
    row = lax.broadcasted_iota(I32, (t, psz), 0)
    col = lax.broadcasted_iota(I32, (t, psz), 1)
    for r in range(PAGES_PER_STEP):
        slot = p * PAGES_PER_STEP + r
        kb = jnp.where(slot < n_pages, page_refs[r][0], knew_ref[0]).astype(BF16)
        sc = _head_scores(q2, w2, kb, t, ih)
        valid = jnp.logical_or(slot < n_pages, jnp.logical_and(slot == n_pages, col <= row))
        key_ref[:, pl.ds(pl.multiple_of(slot * psz, psz), psz)] = jnp.where(valid, _sortable_keys(sc), INT_MIN)

    @pl.when(p == n_steps - 1)
    def _():
        nch = n_steps * PAGES_PER_STEP
        thr = _kth_threshold(key_ref, nch, psz, ktop, t)

        def store(off, val):
            bias_ref[0, :, pl.ds(off, psz)] = val

        _write_bias(key_ref, store, thr, ktop, nch, nch, psz, t, bias_ref.dtype)


def _idx_sample(page_table, q2, w2, pool_ki, knew, *, t, ih, ktop):
    bd, n_pages = page_table.shape
    psz, idim = pool_ki.shape[1], pool_ki.shape[2]
    n_steps = _page_steps(n_pages)
    width = n_steps * PAGES_PER_STEP * psz
    body = functools.partial(_idx_sample_body, t=t, ih=ih, ktop=ktop, n_pages=n_pages, psz=psz,
                             n_steps=n_steps)
    grid_spec = pltpu.PrefetchScalarGridSpec(
        num_scalar_prefetch=1,
        grid=(bd, n_steps),
        in_specs=[pl.BlockSpec((1, t * ih, idim), lambda b, p, pt: (b, 0, 0)),
                  pl.BlockSpec((1, t * ih, 1), lambda b, p, pt: (b, 0, 0))]
                 + _page_specs((1, psz, idim), n_pages)
                 + [pl.BlockSpec((1, psz, idim), lambda b, p, pt: (b, 0, 0))],
        out_specs=pl.BlockSpec((1, t, width), lambda b, p, pt: (b, 0, 0)),
        scratch_shapes=[pltpu.VMEM((t, width), I32)],
    )
    return pl.pallas_call(
        body, grid_spec=grid_spec,
        out_shape=jax.ShapeDtypeStruct((bd, t, width), F32),
        compiler_params=_params("parallel", "arbitrary"),
        name="idx_sample",
    )(page_table, q2, w2, *([pool_ki] * PAGES_PER_STEP), knew)


def _softmax_step(s, v, m_prev, l_prev, acc_prev):
    m_new = jnp.maximum(m_prev, jnp.max(s, axis=-1, keepdims=True))
    alpha = jnp.exp(m_prev - m_new)
    p = jnp.exp(s - m_new)
    l_new = alpha * l_prev + jnp.sum(p, axis=-1, keepdims=True)
    acc_new = alpha * acc_prev + jnp.dot(p.astype(BF16), v, preferred_element_type=F32)
    return m_new, l_new, acc_new


def _attn_prompt_body(q_ref, k_ref, v_ref, b_ref, o_ref, m_ref, l_ref, acc_ref, *, nh, kvh, hd):
    i = pl.program_id(0)
    j = pl.program_id(1)
    grp = nh // kvh

    @pl.when(j == 0)
    def _():
        m_ref[...] = jnp.full(m_ref.shape, -jnp.inf, F32)
        l_ref[...] = jnp.zeros(l_ref.shape, F32)
        acc_ref[...] = jnp.zeros(acc_ref.shape, F32)

    @pl.when(j <= i)
    def _():
        bias = b_ref[...].astype(F32)
        for n in range(kvh):
            kn = k_ref[:, n * hd:(n + 1) * hd].astype(BF16)
            vn = v_ref[:, n * hd:(n + 1) * hd].astype(BF16)
            for g in range(grp):
                h = n * grp + g
                hs = slice(h * hd, (h + 1) * hd)
                s = lax.dot_general(q_ref[:, hs], kn, (((1,), (1,)), ((), ())),
                                    preferred_element_type=F32) + bias
                m_ref[h], l_ref[h], acc_ref[:, hs] = _softmax_step(
                    s, vn, m_ref[h], l_ref[h], acc_ref[:, hs])

    @pl.when(j == i)
    def _():
        for h in range(nh):
            hs = slice(h * hd, (h + 1) * hd)
            o_ref[:, hs] = (acc_ref[:, hs] / l_ref[h]).astype(o_ref.dtype)


def _attn_prompt(q, kv, bias, *, s_len, nh, kvh, hd, tq):
    aw, kvw = nh * hd, kvh * hd
    nt = s_len // tq
    body = functools.partial(_attn_prompt_body, nh=nh, kvh=kvh, hd=hd)
    return pl.pallas_call(
        body,
        grid=(nt, nt),
        in_specs=[pl.BlockSpec((tq, aw), lambda i, j: (i, 0)),
                  pl.BlockSpec((tq, kvw), lambda i, j: (jnp.minimum(i, j), 0)),
                  pl.BlockSpec((tq, kvw), lambda i, j: (jnp.minimum(i, j), 1)),
                  pl.BlockSpec((tq, tq), lambda i, j: (i, jnp.minimum(i, j)))],
        out_specs=pl.BlockSpec((tq, aw), lambda i, j: (i, 0)),
        out_shape=jax.ShapeDtypeStruct((s_len, aw), BF16),
        scratch_shapes=[pltpu.VMEM((nh, tq, 1), F32), pltpu.VMEM((nh, tq, 1), F32),
                        pltpu.VMEM((tq, aw), F32)],
        compiler_params=_params("parallel", "arbitrary"),
        name="attn_prompt",
    )(q, kv, kv, bias)


def _attn_sample_body(pt_ref, q_ref, *rest, t, nh, kvh, hd, n_pages, n_steps):
    kp_refs = rest[:PAGES_PER_STEP]
    vp_refs = rest[PAGES_PER_STEP:2 * PAGES_PER_STEP]
    kn_ref, vn_ref, b_ref, o_ref, m_ref, l_ref, acc_ref = rest[2 * PAGES_PER_STEP:]
    p = pl.program_id(1)
    grp = nh // kvh

    @pl.when(p == 0)
    def _():
        m_ref[...] = jnp.full(m_ref.shape, -jnp.inf, F32)
        l_ref[...] = jnp.zeros(l_ref.shape, F32)
        acc_ref[...] = jnp.zeros(acc_ref.shape, F32)

    bias = jnp.concatenate([b_ref[0]] * grp, axis=0)
    q = q_ref[0]
    for n in range(kvh):
        qn = jnp.concatenate([q[:, (n * grp + g) * hd:(n * grp + g + 1) * hd] for g in range(grp)],
                             axis=0).astype(BF16)
        ks, vs = [], []
        for r in range(PAGES_PER_STEP):
            past = p * PAGES_PER_STEP + r < n_pages
            ks.append(jnp.where(past, kp_refs[r][0, :, n, :], kn_ref[0, :, n * hd:(n + 1) * hd]))
            vs.append(jnp.where(past, vp_refs[r][0, :, n, :], vn_ref[0, :, n * hd:(n + 1) * hd]))
        kn = jnp.concatenate(ks, axis=0).astype(BF16)
        vn = jnp.concatenate(vs, axis=0).astype(BF16)
        s = lax.dot_general(qn, kn, (((1,), (1,)), ((), ())), preferred_element_type=F32) + bias
        m_ref[n], l_ref[n], acc_ref[n] = _softmax_step(s, vn, m_ref[n], l_ref[n], acc_ref[n])

    @pl.when(p == n_steps - 1)
    def _():
        for n in range(kvh):
            on = acc_ref[n] / l_ref[n]
            for g in range(grp):
                h = n * grp + g
                o_ref[0, :, h * hd:(h + 1) * hd] = on[g * t:(g + 1) * t].astype(o_ref.dtype)


def _attn_sample(page_table, q, pool_k, pool_v, knew, vnew, bias, *, t, nh, kvh, hd):
    bd, n_pages = page_table.shape
    psz = pool_k.shape[1]
    aw, kvw = nh * hd, kvh * hd
    grp = nh // kvh
    n_steps = _page_steps(n_pages)
    body = functools.partial(_attn_sample_body, t=t, nh=nh, kvh=kvh, hd=hd, n_pages=n_pages,
                             n_steps=n_steps)
    own_map = lambda b, p, pt: (b, 0, 0)
    page_specs = _page_specs((1, psz, kvh, hd), n_pages)
    grid_spec = pltpu.PrefetchScalarGridSpec(
        num_scalar_prefetch=1,
        grid=(bd, n_steps),
        in_specs=[pl.BlockSpec((1, t, aw), own_map)] + page_specs + page_specs
                 + [pl.BlockSpec((1, psz, kvw), own_map),
                    pl.BlockSpec((1, psz, kvw), own_map),
                    pl.BlockSpec((1, t, PAGES_PER_STEP * psz), lambda b, p, pt: (b, 0, p))],
        out_specs=pl.BlockSpec((1, t, aw), own_map),
        scratch_shapes=[pltpu.VMEM((kvh, grp * t, 1), F32), pltpu.VMEM((kvh, grp * t, 1), F32),
                        pltpu.VMEM((kvh, grp * t, hd), F32)],
    )
    return pl.pallas_call(
        body, grid_spec=grid_spec,
        out_shape=jax.ShapeDtypeStruct((bd, t, aw), F32),
        compiler_params=_params("parallel", "arbitrary"),
        name="attn_sample",
    )(page_table, q, *([pool_k] * PAGES_PER_STEP), *([pool_v] * PAGES_PER_STEP), knew, vnew, bias)


def _softplus(x):
    return jnp.maximum(x, 0.0) + jnp.log(1.0 + jnp.exp(-jnp.abs(x)))


def _dn_prep_body(u_ref, prev_ref, hist_ref, cw_ref, ba_ref, al_ref, dtb_ref, y_ref, bg_ref,
                  *, tm, tiles_per_seq, nh, hd, conv_w, eps):
    first = (pl.program_id(0) % tiles_per_seq) == 0
    r8 = lax.broadcasted_iota(I32, (SUBLANES, hd), 0)
    qscale = hd ** -0.5
    for c in range(u_ref.shape[1] // hd):
        cs = slice(c * hd, (c + 1) * hd)
        cur = u_ref[:, cs]
        prev = jnp.where(first, hist_ref[0, :, cs], prev_ref[:, cs])
        y = cur * cw_ref[conv_w - 1:conv_w, cs]
        for k in range(1, conv_w):
            sk = pltpu.roll(cur, k, 0)
            top = jnp.where(r8 < k, pltpu.roll(prev, k, 0), sk[:SUBLANES])
            sk = top if tm == SUBLANES else jnp.concatenate([top, sk[SUBLANES:]], axis=0)
            y = y + sk * cw_ref[conv_w - 1 - k:conv_w - k, cs]
        y = _silu(y)
        if c < 2 * nh:
            y = y * lax.rsqrt(jnp.sum(y * y, axis=-1, keepdims=True) + eps)
            if c < nh:
                y = y * qscale
        y_ref[:, cs] = y
    ba = ba_ref[...]
    lane = lax.broadcasted_iota(I32, ba.shape, 1)
    decay = -jnp.exp(al_ref[...]) * _softplus(ba + dtb_ref[...])
    bg_ref[...] = jnp.where(lane < nh, _sigmoid(ba), jnp.where(lane < 2 * nh, decay, 0.0))


def _dn_prep(u, hist8, cw8, ba, al_row, dtb_row, *, row_off, nrows, tm, seq_len, nh, hd):
    cch = u.shape[1]
    tps = seq_len // tm
    body = functools.partial(_dn_prep_body, tm=tm, tiles_per_seq=tps, nh=nh, hd=hd,
                             conv_w=CFG["conv_w"], eps=CFG["eps"])
    ob, o8 = row_off // tm, row_off // SUBLANES
    return pl.pallas_call(
        body,
        grid=(nrows // tm,),
        in_specs=[pl.BlockSpec((tm, cch), lambda i: (ob + i, 0)),
                  pl.BlockSpec((SUBLANES, cch),
                               lambda i: (jnp.maximum(o8 + i * (tm // SUBLANES) - 1, 0), 0)),
                  pl.BlockSpec((1, SUBLANES, cch), lambda i: (i // tps, 0, 0)),
                  pl.BlockSpec((SUBLANES, cch), lambda i: (0, 0)),
                  pl.BlockSpec((tm, LANES), lambda i: (ob + i, 0)),
                  pl.BlockSpec((1, LANES), lambda i: (0, 0)),
                  pl.BlockSpec((1, LANES), lambda i: (0, 0))],
        out_specs=[pl.BlockSpec((tm, cch), lambda i: (i, 0)),
                   pl.BlockSpec((tm, LANES), lambda i: (i, 0))],
        out_shape=[jax.ShapeDtypeStruct((nrows, cch), F32),
                   jax.ShapeDtypeStruct((nrows, LANES), F32)],
        compiler_params=_params("parallel"),
        name="dn_prep",
    )(u, u, hist8, cw8, ba, al_row, dtb_row)


def _dot_hi(a, b):
    return jnp.dot(a, b, precision=HIGHEST, preferred_element_type=F32)


def _dot_nt_hi(a, b):
    return lax.dot_general(a, b, (((1,), (1,)), ((), ())), precision=HIGHEST,
                           preferred_element_type=F32)


def _dot_tn_hi(a, b):
    return lax.dot_general(a, b, (((0,), (0,)), ((), ())), precision=HIGHEST,
                           preferred_element_type=F32)


def _dn_scan_body(y_ref, bg_ref, z_ref, s0_ref, ng_ref, o_ref, s_ref, *, c, nh, hd, eps):
    @pl.when(pl.program_id(1) == 0)
    def _():
        s_ref[...] = s0_ref[...]

    bg = bg_ref[...]
    ri = lax.broadcasted_iota(I32, (c, c), 0)
    ci = lax.broadcasted_iota(I32, (c, c), 1)
    causal = ri >= ci
    strict = ri > ci
    eye = jnp.where(ri == ci, 1.0, 0.0)
    gcum = _dot_hi(jnp.where(causal, 1.0, 0.0), bg)
    gpad = gcum if c == LANES else jnp.concatenate([gcum, jnp.zeros((LANES - c, LANES), F32)], axis=0)
    g_t = gpad.T
    n_double = int(math.log2(c)) - 1
    for h in range(nh):
        q = y_ref[:, h * hd:(h + 1) * hd]
        k = y_ref[:, (nh + h) * hd:(nh + h + 1) * hd]
        v = y_ref[:, (2 * nh + h) * hd:(2 * nh + h + 1) * hd]
        beta = bg[:, h:h + 1]
        gcol = gcum[:, nh + h:nh + h + 1]
        grow = g_t[nh + h:nh + h + 1, :c]
        glast = gcum[c - 1:c, nh + h:nh + h + 1]
        decay = jnp.where(causal, jnp.exp(jnp.where(causal, gcol - grow, 0.0)), 0.0)
        a = jnp.where(strict, beta * _dot_nt_hi(k, k) * decay, 0.0)
        p = -a
        tinv = eye + p
        for _ in range(n_double):
            p = _dot_hi(p, p)
            tinv = tinv + _dot_hi(tinv, p)
        eg = jnp.exp(gcol)
        u = _dot_hi(tinv, beta * v)
        w = _dot_hi(tinv, (beta * eg) * k)
        qk = _dot_nt_hi(q, k) * decay
        s = s_ref[0, h]
        e = u - _dot_hi(w, s)
        o = _dot_hi(q * eg, s) + _dot_hi(qk, e)
        s_ref[0, h] = s * jnp.exp(glast) + _dot_tn_hi(k * jnp.exp(glast - gcol), e)
        on = o * lax.rsqrt(jnp.mean(o * o, axis=-1, keepdims=True) + eps) * ng_ref[...]
        o_ref[:, h * hd:(h + 1) * hd] = (on * _silu(z_ref[:, h * hd:(h + 1) * hd])).astype(o_ref.dtype)


def _dn_scan(y, bg, z, s0, ng_row, *, z_row_off, c, seq_len, nh, hd):
    nseq = s0.shape[0]
    nch = seq_len // c
    zo = z_row_off // c
    body = functools.partial(_dn_scan_body, c=c, nh=nh, hd=hd, eps=CFG["eps"])
    return pl.pallas_call(
        body,
        grid=(nseq, nch),
        in_specs=[pl.BlockSpec((c, y.shape[1]), lambda b, n: (b * nch + n, 0)),
                  pl.BlockSpec((c, LANES), lambda b, n: (b * nch + n, 0)),
                  pl.BlockSpec((c, nh * hd), lambda b, n: (zo + b * nch + n, 0)),
                  pl.BlockSpec((1, nh, hd, hd), lambda b, n: (b, 0, 0, 0)),
                  pl.BlockSpec((1, hd), lambda b, n: (0, 0))],
        out_specs=[pl.BlockSpec((c, nh * hd), lambda b, n: (b * nch + n, 0)),
                   pl.BlockSpec((1, nh, hd, hd), lambda b, n: (b, 0, 0, 0))],
        out_shape=[jax.ShapeDtypeStruct((nseq * seq_len, nh * hd), F32),
                   jax.ShapeDtypeStruct(s0.shape, F32)],
        compiler_params=_params("parallel", "arbitrary"),
        name="dn_scan",
    )(y, bg, z, s0, ng_row)


def _first_argmax(x, lane, width):
    m = jnp.max(x, axis=-1, keepdims=True)
    return jnp.min(jnp.where(x == m, lane, width), axis=-1, keepdims=True)


def _router_body(h_ref, wr_ref, rb_ref, te_ref, tr_ref, tw_ref, rkt_ref, cnt_ref,
                 *, tm, ne, n_groups, topk_groups, top_k, scale):
    @pl.when(pl.program_id(0) == 0)
    def _():
        cnt_ref[...] = jnp.zeros(cnt_ref.shape, F32)

    logits = jnp.dot(h_ref[...].astype(BF16), wr_ref[...], preferred_element_type=F32)
    scores = _sigmoid(logits)
    biased = scores + rb_ref[...]
    lane = lax.broadcasted_iota(I32, (tm, ne), 1)
    lane_grp = lax.shift_right_logical(lane, int(math.log2(ne // n_groups)))
    ninf = -jnp.inf
    gscore = jnp.full((tm, ne), ninf, F32)
    for g in range(n_groups):
        xg = jnp.where(lane_grp == g, biased, ninf)
        m1 = jnp.max(xg, axis=-1, keepdims=True)
        i1 = jnp.min(jnp.where(xg == m1, lane, ne), axis=-1, keepdims=True)
        m2 = jnp.max(jnp.where(lane == i1, ninf, xg), axis=-1, keepdims=True)
        gscore = jnp.where(lane == g, m1 + m2, gscore)
    gsel = jnp.zeros((tm, ne), F32)
    for _ in range(topk_groups):
        ig = _first_argmax(gscore, lane, ne)
        gsel = jnp.where(lane_grp == ig, 1.0, gsel)
        gscore = jnp.where(lane == ig, ninf, gscore)
    x = jnp.where(gsel > 0.0, biased, ninf)
    sel = jnp.zeros((tm, ne), F32)
    idxs, ws = [], []
    for _ in range(top_k):
        ie = _first_argmax(x, lane, ne)
        hit = lane == ie
        idxs.append(ie)
        ws.append(jnp.sum(jnp.where(hit, scores, 0.0), axis=-1, keepdims=True))
        sel = jnp.where(hit, 1.0, sel)
        x = jnp.where(hit, ninf, x)
    wsum = functools.reduce(lambda p, q: p + q, ws)
    r = lax.broadcasted_iota(I32, (tm, tm), 0)
    cc = lax.broadcasted_iota(I32, (tm, tm), 1)
    earlier = jnp.where(r > cc, 1.0, 0.0).astype(BF16)
    rank = cnt_ref[...] + jnp.dot(earlier, sel.astype(BF16), preferred_element_type=F32)
    cnt_ref[...] += jnp.sum(sel, axis=0, keepdims=True)
    te = jnp.zeros((tm, ne), I32)
    tr = jnp.zeros((tm, ne), F32)
    tw = jnp.zeros((tm, ne), F32)
    for kk in range(top_k):
        slot = lane == kk
        te = jnp.where(slot, idxs[kk], te)
        tr = jnp.where(slot, jnp.sum(jnp.where(lane == idxs[kk], rank, 0.0), axis=-1, keepdims=True), tr)
        tw = jnp.where(slot, ws[kk] / wsum * scale, tw)
    te_ref[...] = te
    tr_ref[...] = tr
    tw_ref[...] = tw
    rkt_ref[...] = jnp.where(sel > 0.0, rank, -1.0).T


def _router(h2, w_router, rb_row, *, tm):
    m, d = h2.shape
    ne = w_router.shape[1]
    assert ne == LANES
    body = functools.partial(_router_body, tm=tm, ne=ne, n_groups=CFG["n_groups"],
                             topk_groups=CFG["topk_groups"], top_k=CFG["top_k"],
                             scale=CFG["routed_scale"])
    tile = pl.BlockSpec((tm, ne), lambda i: (i, 0))
    return pl.pallas_call(
        body,
        grid=(m // tm,),
        in_specs=[pl.BlockSpec((tm, d), lambda i: (i, 0)),
                  pl.BlockSpec((d, ne), lambda i: (0, 0)),
                  pl.BlockSpec((1, ne), lambda i: (0, 0))],
        out_specs=[tile, tile, tile, pl.BlockSpec((ne, tm), lambda i: (0, i)),
                   pl.BlockSpec((1, ne), lambda i: (0, 0))],
        out_shape=[jax.ShapeDtypeStruct((m, ne), I32), jax.ShapeDtypeStruct((m, ne), F32),
                   jax.ShapeDtypeStruct((m, ne), F32), jax.ShapeDtypeStruct((ne, m), F32),
                   jax.ShapeDtypeStruct((1, ne), F32)],
        compiler_params=_params("arbitrary"),
        name="router",
    )(h2, w_router, rb_row)


def _dest_body(te_ref, tr_ref, ps_ref, d_ref, *, top_k):
    te = te_ref[...]
    tr = tr_ref[...]
    lane = lax.broadcasted_iota(I32, te.shape, 1)
    d = jnp.zeros(te.shape, F32)
    for kk in range(top_k):
        base = jnp.sum(jnp.where(lane == te[:, kk:kk + 1], ps_ref[...], 0.0), axis=-1, keepdims=True)
        d = jnp.where(lane == kk, base + tr[:, kk:kk + 1], d)
    d_ref[...] = d.astype(I32)


def _dest(te, tr, pad_start_row, *, tm):
    m, ne = te.shape
    tile = pl.BlockSpec((tm, ne), lambda i: (i, 0))
    return pl.pallas_call(
        functools.partial(_dest_body, top_k=CFG["top_k"]),
        grid=(m // tm,),
        in_specs=[tile, tile, pl.BlockSpec((1, ne), lambda i: (0, 0))],
        out_specs=tile,
        out_shape=jax.ShapeDtypeStruct((m, ne), I32),
        compiler_params=_params("parallel"),
        name="dest",
    )(te, tr, pad_start_row)


def _row_tokens_body(sbe_ref, off_ref, rk_ref, tok_ref, *, sb, sub):
    rk = rk_ref[0]
    tok = lax.broadcasted_iota(I32, rk.shape, 1).astype(F32)
    base = off_ref[pl.program_id(0)]
    for r0 in range(0, sb, sub):
        target = (base + r0 + lax.broadcasted_iota(I32, (sub, 1), 0)).astype(F32)
        tok_ref[r0:r0 + sub, :] = jnp.sum(jnp.where(rk == target, tok, 0.0), axis=-1,
                                          keepdims=True).astype(I32)


def _row_tokens(sb_e, sb_off, rkt, *, sb):
    ne, m = rkt.shape
    n_sb = sb_e.shape[0]
    grid_spec = pltpu.PrefetchScalarGridSpec(
        num_scalar_prefetch=2,
        grid=(n_sb,),
        in_specs=[pl.BlockSpec((1, 1, m), lambda s, sbe, off: (sbe[s], 0, 0))],
        out_specs=pl.BlockSpec((sb, 1), lambda s, sbe, off: (s, 0)),
    )
    return pl.pallas_call(
        functools.partial(_row_tokens_body, sb=sb, sub=min(sb, LANES)), grid_spec=grid_spec,
        out_shape=jax.ShapeDtypeStruct((n_sb * sb, 1), I32),
        compiler_params=_params("parallel"),
        name="moe_row_tokens",
    )(sb_e, sb_off, rkt.reshape(ne, 1, m))


def _gather_rows(ids_ref, src_hbm, dst, sem, n):
    def body(r, _):
        pltpu.make_async_copy(src_hbm.at[pl.ds(ids_ref[r], 1)], dst.at[pl.ds(r, 1)], sem).start()
        return 0
    lax.fori_loop(0, n, body, 0)


def _wait_rows(src_hbm, dst, sem, n):
    def body(r, _):
        pltpu.make_async_copy(src_hbm.at[pl.ds(0, 1)], dst.at[pl.ds(r, 1)], sem).wait()
        return 0
    lax.fori_loop(0, n, body, 0)


def _experts_body(sbe_ref, nu_ref, tok_ref, tok_next_ref, x_hbm, wg_ref, wu_ref, wd_ref, y_ref,
                  stage, xb16, wgu, sem, *, sb, ce, sub):
    s = pl.program_id(0)
    c = pl.program_id(1)
    used = s < nu_ref[0]

    @pl.when(jnp.logical_and(used, c == 0))
    def _():
        @pl.when(s == 0)
        def _():
            _gather_rows(tok_ref, x_hbm, stage, sem, sb)

        _wait_rows(x_hbm, stage, sem, sb)
        xb16[...] = stage[...].astype(BF16)

        @pl.when(s + 1 < nu_ref[0])
        def _():
            _gather_rows(tok_next_ref, x_hbm, stage, sem, sb)

    @pl.when(used)
    def _():
        wgu[:, :ce] = wg_ref[0].astype(BF16)
        wgu[:, ce:] = wu_ref[0].astype(BF16)
        wdn = wd_ref[0].astype(BF16)
        for r0 in range(0, sb, sub):
            gu = jnp.dot(xb16[r0:r0 + sub, :], wgu[...], preferred_element_type=F32)
            hmid = (_silu(gu[:, :ce]) * gu[:, ce:]).astype(BF16)
            y = jnp.dot(hmid, wdn, preferred_element_type=F32)

            @pl.when(c == 0)
            def _():
                y_ref[r0:r0 + sub, :] = y

            @pl.when(c > 0)
            def _():
                y_ref[r0:r0 + sub, :] += y

    @pl.when(jnp.logical_and(jnp.logical_not(used), c == 0))
    def _():
        y_ref[...] = jnp.zeros(y_ref.shape, F32)


def _experts(sb_e, n_used, row_tok, x, w_gate, w_up, w_down, *, sb, ce):
    d = x.shape[1]
    n_sb = sb_e.shape[0]
    nc = w_gate.shape[2] // ce

    def s_eff(s, nu):
        return jnp.minimum(s, nu[0] - 1)

    def c_eff(s, c, nu):
        return jnp.where(s < nu[0], c, nc - 1)

    grid_spec = pltpu.PrefetchScalarGridSpec(
        num_scalar_prefetch=2,
        grid=(n_sb, nc),
        in_specs=[pl.BlockSpec((sb,), lambda s, c, sbe, nu: (s_eff(s, nu),), memory_space=pltpu.SMEM),
                  pl.BlockSpec((sb,), lambda s, c, sbe, nu: (jnp.minimum(s + 1, nu[0] - 1),),
                               memory_space=pltpu.SMEM),
                  pl.BlockSpec(memory_space=pl.ANY),
                  pl.BlockSpec((1, d, ce), lambda s, c, sbe, nu: (sbe[s_eff(s, nu)], 0, c_eff(s, c, nu))),
                  pl.BlockSpec((1, d, ce), lambda s, c, sbe, nu: (sbe[s_eff(s, nu)], 0, c_eff(s, c, nu))),
                  pl.BlockSpec((1, ce, d), lambda s, c, sbe, nu: (sbe[s_eff(s, nu)], c_eff(s, c, nu), 0))],
        out_specs=pl.BlockSpec((sb, d), lambda s, c, sbe, nu: (s, 0)),
        scratch_shapes=[pltpu.VMEM((sb, d), F32), pltpu.VMEM((sb, d), BF16),
                        pltpu.VMEM((d, 2 * ce), BF16), pltpu.SemaphoreType.DMA(())],
    )
    return pl.pallas_call(
        functools.partial(_experts_body, sb=sb, ce=ce, sub=min(sb, 256)), grid_spec=grid_spec,
        out_shape=jax.ShapeDtypeStruct((n_sb * sb, d), F32),
        compiler_params=_params("arbitrary", "arbitrary"),
        name="moe_experts",
    )(sb_e, n_used, row_tok, row_tok, x, w_gate, w_up, w_down)


def _combine_body(dest_ref, dest_next_ref, ys_hbm, w_ref, sh_ref, x1_ref, g2_ref, o_ref, buf, sem,
                  *, tc, top_k, n_steps):
    i = pl.program_id(0)
    slot = lax.rem(i, 2)
    n = tc * top_k

    def gather(ids_ref, sl):
        def body(a, _):
            pltpu.make_async_copy(ys_hbm.at[pl.ds(ids_ref[a], 1)],
                                  buf.at[sl, a // top_k, pl.ds(a % top_k, 1)], sem.at[sl]).start()
            return 0
        lax.fori_loop(0, n, body, 0)

    @pl.when(i == 0)
    def _():
        gather(dest_ref, 0)

    @pl.when(i + 1 < n_steps)
    def _():
        gather(dest_next_ref, 1 - slot)

    def wait(a, _):
        pltpu.make_async_copy(ys_hbm.at[pl.ds(0, 1)], buf.at[slot, a // top_k, pl.ds(a % top_k, 1)],
                              sem.at[slot]).wait()
        return 0

    lax.fori_loop(0, n, wait, 0)
    routed = jnp.sum(buf[slot] * w_ref[...], axis=1)
    o_ref[...] = x1_ref[...] + g2_ref[...] * (routed + sh_ref[...])


def _combine(dest_flat, ys, w3, shared, x1, gate2, *, tc, tmod, n_prompt_tiles):
    top_k = CFG["top_k"]
    m, d = x1.shape
    n_steps = m // tc
    per = tmod // tc
    npt = n_prompt_tiles * per

    def g2_map(i):
        return (jnp.where(i >= npt, per, 0) + i % per, 0)

    row = pl.BlockSpec((tc, d), lambda i: (i, 0))
    return pl.pallas_call(
        functools.partial(_combine_body, tc=tc, top_k=top_k, n_steps=n_steps),
        grid=(n_steps,),
        in_specs=[pl.BlockSpec((tc * top_k,), lambda i: (i,), memory_space=pltpu.SMEM),
                  pl.BlockSpec((tc * top_k,), lambda i: (jnp.minimum(i + 1, n_steps - 1),),
                               memory_space=pltpu.SMEM),
                  pl.BlockSpec(memory_space=pl.ANY),
                  pl.BlockSpec((tc, top_k, 1), lambda i: (i, 0, 0)),
                  row, row, pl.BlockSpec((tc, d), g2_map)],
        out_specs=row,
        out_shape=jax.ShapeDtypeStruct((m, d), F32),
        scratch_shapes=[pltpu.VMEM((2, tc, top_k, d), F32), pltpu.SemaphoreType.DMA((2,))],
        compiler_params=_params("arbitrary"),
        name="moe_combine",
    )(dest_flat, dest_flat, ys, w3, shared, x1, gate2)


SUPER_BLOCK = 512
EXPERT_CHUNK = 128
ROW_TILES = (768, 512, 256, 128, 64, 32, 16)
COL_TILES = (512, 256, 128)


def _pad_rows(a, n, axis, front=False):
    pad = [(0, 0)] * a.ndim
    pad[axis] = (n - a.shape[axis], 0) if front else (0, n - a.shape[axis])
    return jnp.pad(a, pad)


def _lane_row(vec, offset):
    return jnp.zeros((1, LANES), F32).at[0, offset:offset + vec.shape[0]].set(vec.astype(F32))


def _moe(h2, x1, gate2, lw, *, tmod, npt, tmm):
    m, d = h2.shape
    top_k = CFG["top_k"]
    ne = lw["w_router"].shape[1]
    te, tr, tw, rkt, counts = _router(h2, lw["w_router"].astype(BF16),
                                      lw["router_bias"][None].astype(F32), tm=tmod)
    sb = SUPER_BLOCK
    n_sb = -(-(m * top_k + ne * (sb - 1)) // sb)
    cnt = counts[0].astype(I32)
    padded = (cnt + sb - 1) // sb * sb
    pad_end = jnp.cumsum(padded)
    pad_start = pad_end - padded
    n_used = (pad_end[-1:] // sb).astype(I32)
    sb_row0 = jnp.arange(n_sb, dtype=I32) * sb
    sb_e = jnp.minimum(jnp.sum((pad_end[None, :] <= sb_row0[:, None]).astype(I32), axis=1), ne - 1)
    dest = _dest(te, tr, pad_start.astype(F32)[None], tm=tmod)[:, :top_k].reshape(-1)
    row_tok = _row_tokens(sb_e, sb_row0 - pad_start[sb_e], rkt, sb=sb).reshape(-1)
    ys = _experts(sb_e, n_used, row_tok, h2, lw["w_e_gate"], lw["w_e_up"], lw["w_e_down"],
                  sb=sb, ce=min(EXPERT_CHUNK, lw["w_e_gate"].shape[2]))
    tns = _pick(lw["w_s_gate"].shape[1], COL_TILES)
    gs = _mm([h2], lw["w_s_gate"].astype(BF16), tm=tmm, tn=tns, name="shared_gate")
    us = _mm([h2], lw["w_s_up"].astype(BF16), tm=tmm, tn=tns, name="shared_up")
    shared = _mm([gs, us], lw["w_s_down"].astype(BF16), tm=tmm, tn=_pick(d, COL_TILES),
                 prologue=lambda g, u: _silu(g) * u, name="shared_down")
    return _combine(dest, ys, tw[:, :top_k].reshape(m, top_k, 1), shared, x1, gate2,
                    tc=min(32, tmod), tmod=tmod, n_prompt_tiles=npt)


def _layer(xp, xs, cp, cs, pool_k, pool_v, pool_ki, st_conv, st_delta, page_table, lw):
    eps = CFG["eps"]
    bp, s_len, d = xp.shape
    bd, t, _ = xs.shape
    assert bp == 1 and t % SUBLANES == 0 and t >= CFG["conv_w"] - 1
    n_s = bd * t
    m = s_len + n_s
    tmod = n_s
    assert s_len % tmod == 0
    npt = s_len // tmod
    tmm = _pick(m, ROW_TILES)
    hd = lw["q_norm_g"].shape[0]
    aw = lw["w_br_a"].shape[0]
    nh = aw // hd
    n_phys, psz, kvh, _ = pool_k.shape
    kvw = kvh * hd
    idim = pool_ki.shape[2]
    dnh = lw["dn_a_log"].shape[0]
    dv = lw["dn_norm_g"].shape[0]
    dnw = lw["w_br_b"].shape[0]
    cch = lw["conv_w"].shape[1]
    assert cch == 3 * dnw and dnw == dnh * dv
    n_in = lw["w_in"].shape[1]
    ih = (n_in - (aw + 2 * kvw + idim + cch + dnw + 2 * dnh + 2 * d)) // (idim + 1)
    assert idim + ih <= LANES and 2 * dnh <= LANES

    n_c = bp + bd
    c_all = _pad_rows(jnp.concatenate([cp, cs], axis=0), -(-n_c // 16) * 16, 0)
    mod = _mm([c_all], lw["w_ada"], tm=c_all.shape[0], tn=_pick(6 * d, COL_TILES),
              prologue=_silu, epilogue=_ep_bias, extras=[lw["b_ada"][None]], name="ada")

    def expand(a):
        return jnp.concatenate([jnp.broadcast_to(a[0:1], (tmod, d)),
                                jnp.repeat(a[bp:bp + bd], t, axis=0)], axis=0)

    shift1, scale1, gate1, shift2, scale2, gate2 = (expand(a) for a in jnp.split(mod, 6, axis=1))
    x_all = jnp.concatenate([xp.reshape(s_len, d), xs.reshape(n_s, d)], axis=0)
    h = _normmod(x_all, lw["norm1_g"][None], scale1, shift1, tmod=tmod, n_prompt_tiles=npt,
                 out_dtype=BF16)

    offs = [0]
    for wdt in (aw, kvw, kvw, ih * idim, idim, ih, cch, dnw, dnh, dnh, d, d):
        offs.append(offs[-1] + wdt)
    w_in = lw["w_in"]

    def cols(a, b, pad_to=None):
        w = w_in[:, offs[a]:offs[b]].astype(BF16)
        return w if pad_to is None else _pad_rows(w, pad_to, 1)

    qn = _mm([h], cols(0, 1), tm=tmm, tn=_pick(aw, COL_TILES), out_dtype=BF16, name="proj_q",
             epilogue=functools.partial(_ep_qnorm, hd=hd, eps=eps, scale=hd ** -0.5),
             extras=[jnp.tile(lw["q_norm_g"], nh)[None]])
    tkv = _pick(kvw, COL_TILES)
    kv = _mm([h], cols(1, 3), tm=tmm, tn=tkv, name="proj_kv",
             epilogue=functools.partial(_ep_kvnorm, hd=hd, eps=eps, n_k_tiles=kvw // tkv),
             extras=[jnp.concatenate([jnp.tile(lw["k_norm_g"], kvh), jnp.ones((kvw,), F32)])[None]])
    qi = _mm([h], cols(3, 4), tm=tmm, tn=_pick(ih * idim, COL_TILES), out_dtype=BF16, name="proj_qi")
    kw = _mm([h], cols(4, 6, LANES), tm=tmm, tn=LANES, name="proj_kw",
             epilogue=functools.partial(_ep_kw, idx_dim=idim, eps=eps, wscale=(ih * idim) ** -0.5),
             extras=[_lane_row(lw["kidx_norm_g"], 0)])
    u = _mm([h], cols(6, 7), tm=tmm, tn=_pick(cch, COL_TILES), name="proj_u")
    z = _mm([h], cols(7, 8), tm=tmm, tn=_pick(dnw, COL_TILES), name="proj_z")
    ba = _mm([h], cols(8, 10, LANES), tm=tmm, tn=LANES, name="proj_ba")
    gates = _mm([h], cols(10, 12), tm=tmm, tn=_pick(d, COL_TILES), epilogue=_ep_sigmoid, name="proj_g")

    ki_all = kw[:, :idim]
    wi_all = kw[:, idim:idim + ih]
    bias_p = _idx_prompt(qi[:s_len].reshape(s_len * ih, idim), wi_all[:s_len].reshape(s_len * ih, 1),
                         ki_all[:s_len].astype(BF16), s_len=s_len, ih=ih,
                         ktop=min(CFG["index_topk"], s_len // 4),
                         tq=_pick(s_len, (128, 64, 32, 16)), tk=_pick(s_len, (256, 128)))
    oa_p = _attn_prompt(qn, kv, bias_p, s_len=s_len, nh=nh, kvh=kvh, hd=hd, tq=_pick(s_len, (256, 128)))
    n_pages = page_table.shape[1]
    bias_s = _idx_sample(page_table, qi[s_len:].reshape(bd, t * ih, idim),
                         wi_all[s_len:].reshape(bd, t * ih, 1), pool_ki,
                         _pad_rows(ki_all[s_len:].reshape(bd, t, idim), psz, 1),
                         t=t, ih=ih, ktop=min(CFG["index_topk"], (n_pages * psz + t) // 4))
    kv_s = kv[s_len:].reshape(bd, t, 2 * kvw)
    oa_s = _attn_sample(page_table, qn[s_len:].astype(F32).reshape(bd, t, aw),
                        pool_k, pool_v,
                        _pad_rows(kv_s[:, :, :kvw], psz, 1), _pad_rows(kv_s[:, :, kvw:], psz, 1),
                        bias_s, t=t, nh=nh, kvh=kvh, hd=hd)
    o_a = jnp.concatenate([oa_p, oa_s.reshape(n_s, aw).astype(BF16)], axis=0)

    cw8 = _pad_rows(lw["conv_w"], SUBLANES, 0)
    al_row = _lane_row(lw["dn_a_log"], dnh)
    dtb_row = _lane_row(lw["dn_dt_bias"], dnh)
    ng_row = lw["dn_norm_g"][None]
    y_p, bg_p = _dn_prep(u, jnp.zeros((bp, SUBLANES, cch), F32), cw8, ba, al_row, dtb_row,
                         row_off=0, nrows=s_len, tm=_pick(s_len, (128, 64, 32, 16, 8)),
                         seq_len=s_len, nh=dnh, hd=dv)
    y_s, bg_s = _dn_prep(u, _pad_rows(st_conv.astype(F32), SUBLANES, 1, front=True), cw8, ba, al_row,
                         dtb_row, row_off=s_len, nrows=n_s, tm=SUBLANES, seq_len=t, nh=dnh, hd=dv)
    chunk = CFG["dn_chunk"]
    ob_p, sd_p = _dn_scan(y_p, bg_p, z, jnp.zeros((bp,) + st_delta.shape[1:], F32), ng_row,
                          z_row_off=0, c=chunk if s_len % chunk == 0 else s_len, seq_len=s_len,
                          nh=dnh, hd=dv)
    ob_s, sd_s = _dn_scan(y_s, bg_s, z, st_delta.astype(F32), ng_row,
                          z_row_off=s_len, c=chunk if t % chunk == 0 else t, seq_len=t, nh=dnh, hd=dv)
    o_b = jnp.concatenate([ob_p, ob_s], axis=0)

    tnd = _pick(d, COL_TILES)
    ma = _mm([o_a], lw["w_br_a"].astype(BF16), tm=tmm, tn=tnd, name="branch_a",
             epilogue=lambda acc, j, g: acc * g,
             extras=[(gates, (tmm, tnd), lambda i, j, k: (i, j))])
    merged = _mm([o_b], lw["w_br_b"].astype(BF16), tm=tmm, tn=tnd, out_dtype=BF16, name="branch_b",
                 epilogue=lambda acc, j, g, other: acc * g + other,
                 extras=[(gates, (tmm, tnd), lambda i, j, k: (i, j + d // tnd)), ma])
    x1 = _mm([merged], lw["w_out"].astype(BF16), tm=tmod, tn=tnd, name="out_proj",
             epilogue=lambda acc, j, x, g1: x + g1 * acc,
             extras=[x_all, (gate1, (tmod, tnd), lambda i, j, k: (jnp.where(i >= npt, 1, 0), j))])

    h2 = _normmod(x1, lw["norm2_g"][None], scale2, shift2, tmod=tmod, n_prompt_tiles=npt, out_dtype=F32)
    y = _moe(h2, x1, gate2, lw, tmod=tmod, npt=npt, tmm=tmm)

    nb = CFG["conv_w"] - 1
    u_s = u[s_len:].reshape(bd, t, cch)
    state_p = (kv[:s_len, :kvw].reshape(bp, s_len, kvh, hd), kv[:s_len, kvw:].reshape(bp, s_len, kvh, hd),
               ki_all[:s_len].reshape(bp, s_len, idim), u[s_len - nb:s_len][None], sd_p)
    state_s = (kv_s[:, :, :kvw].reshape(bd, t, kvh, hd), kv_s[:, :, kvw:].reshape(bd, t, kvh, hd),
               ki_all[s_len:].reshape(bd, t, idim), u_s[:, t - nb:], sd_s)
    return y[:s_len].reshape(bp, s_len, d), y[s_len:].reshape(bd, t, d), state_p, state_s


_LAYER_WEIGHTS = ("w_ada", "b_ada", "norm1_g", "w_in", "q_norm_g", "k_norm_g", "kidx_norm_g", "conv_w",
                  "dn_a_log", "dn_dt_bias", "dn_norm_g", "w_br_a", "w_br_b", "w_out", "norm2_g",
                  "w_router", "router_bias", "w_e_gate", "w_e_up", "w_e_down", "w_s_gate", "w_s_up",
                  "w_s_down")


def kernel(x_prompt, x_sample, cache_k, cache_v, cache_kidx, state_conv, state_delta, page_table,
           c_prompt, c_sample, w_ada, b_ada, norm1_g, w_in, q_norm_g, k_norm_g, kidx_norm_g, conv_w,
           dn_a_log, dn_dt_bias, dn_norm_g, w_br_a, w_br_b, w_out, norm2_g, w_router, router_bias,
           w_e_gate, w_e_up, w_e_down, w_s_gate, w_s_up, w_s_down):
    weights = dict(zip(_LAYER_WEIGHTS, (w_ada, b_ada, norm1_g, w_in, q_norm_g, k_norm_g, kidx_norm_g,
                                        conv_w, dn_a_log, dn_dt_bias, dn_norm_g, w_br_a, w_br_b, w_out,
                                        norm2_g, w_router, router_bias, w_e_gate, w_e_up, w_e_down,
                                        w_s_gate, w_s_up, w_s_down)))
    y_p, y_s = x_prompt, x_sample
    states_p, states_s = [], []
    for layer in range(w_ada.shape[0]):
        lw = {name: w[layer] for name, w in weights.items()}
        y_p, y_s, st_p, st_s = _layer(y_p, y_s, c_prompt, c_sample, cache_k[layer], cache_v[layer],
                                      cache_kidx[layer], state_conv[layer], state_delta[layer],
                                      page_table, lw)
        states_p.append(st_p)
        states_s.append(st_s)
    stack = lambda sts, i: jnp.stack([st[i] for st in sts])
    return (y_p, y_s) + tuple(stack(states_p, i) for i in range(5)) + tuple(stack(states_s, i) for i in range(5))
```

```python
import functools
import math

import jax
import jax.numpy as jnp
from jax import lax
from jax.experimental import pallas as pl
from jax.experimental.pallas import tpu as pltpu

F32, BF16, I32 = jnp.float32, jnp.bfloat16, jnp.int32
HIGHEST = lax.Precision.HIGHEST

CFG = dict(
    kv_heads=4, idx_heads=32, index_topk=256, page_size=128,
    dn_chunk=64, conv_w=4, n_groups=8, topk_groups=4, top_k=8,
    routed_scale=2.5, eps=1e-6,
)

LANES = 128
SUBLANES = 8
VMEM_LIMIT = 56 * 1024 * 1024
NEG_BIG = -1e30
INT_MIN = -(2 ** 31)


def _params(*sem):
    return pltpu.CompilerParams(dimension_semantics=sem, vmem_limit_bytes=VMEM_LIMIT)


def _pick(n, cands):
    for c in cands:
        if n % c == 0:
            return c
    raise ValueError(f"no tile in {cands} divides {n}")


def _sigmoid(x):
    return 1.0 / (1.0 + jnp.exp(-x))


def _silu(x):
    return x * _sigmoid(x)


def _mm_body(*refs, n_a, n_e, nk, prologue, epilogue):
    a_refs = refs[:n_a]
    b_ref = refs[n_a]
    e_refs = refs[n_a + 1:n_a + 1 + n_e]
    o_ref = refs[n_a + 1 + n_e]
    j = pl.program_id(1)
    a = prologue(*[r[...] for r in a_refs]) if prologue else a_refs[0][...]
    part = jnp.dot(a.astype(BF16), b_ref[...].astype(BF16), preferred_element_type=F32)

    def finish(acc):
        res = epilogue(acc, j, *[e[...] for e in e_refs]) if epilogue else acc
        o_ref[...] = res.astype(o_ref.dtype)

    if nk == 1:
        finish(part)
    else:
        acc_ref = refs[n_a + 2 + n_e]
        k = pl.program_id(2)

        @pl.when(k == 0)
        def _():
            acc_ref[...] = part

        @pl.when(k > 0)
        def _():
            acc_ref[...] += part

        @pl.when(k == nk - 1)
        def _():
            finish(acc_ref[...])


def _mm(a_list, b, *, tm, tn, tk=None, prologue=None, epilogue=None, extras=(),
        out_dtype=F32, name="mm"):
    m, kdim = a_list[0].shape
    n = b.shape[1]
    tk = kdim if tk is None else tk
    nk = kdim // tk
    assert m % tm == 0 and n % tn == 0 and kdim % tk == 0, (m, n, kdim, tm, tn, tk)
    in_specs = [pl.BlockSpec((tm, tk), lambda i, j, k: (i, k)) for _ in a_list]
    in_specs.append(pl.BlockSpec((tk, tn), lambda i, j, k: (k, j)))
    ex_arrays = []
    for e in extras:
        if isinstance(e, tuple):
            arr, bs, im = e
            in_specs.append(pl.BlockSpec(bs, im))
        elif e.shape[0] == 1:
            arr = e
            in_specs.append(pl.BlockSpec((1, tn), lambda i, j, k: (0, j)))
        else:
            arr = e
            in_specs.append(pl.BlockSpec((tm, tn), lambda i, j, k: (i, j)))
        ex_arrays.append(arr)
    body = functools.partial(_mm_body, n_a=len(a_list), n_e=len(ex_arrays), nk=nk,
                             prologue=prologue, epilogue=epilogue)
    scratch = [pltpu.VMEM((tm, tn), F32)] if nk > 1 else []
    return pl.pallas_call(
        body,
        grid=(m // tm, n // tn, nk),
        in_specs=in_specs,
        out_specs=pl.BlockSpec((tm, tn), lambda i, j, k: (i, j)),
        out_shape=jax.ShapeDtypeStruct((m, n), out_dtype),
        scratch_shapes=scratch,
        compiler_params=_params("parallel", "parallel", "arbitrary"),
        name=name,
    )(*a_list, b, *ex_arrays)


def _normmod_body(x_ref, g_ref, sc_ref, sh_ref, o_ref, *, eps):
    x = x_ref[...]
    ms = jnp.mean(x * x, axis=-1, keepdims=True)
    y = x * lax.rsqrt(ms + eps) * g_ref[...]
    o_ref[...] = (y * (1.0 + sc_ref[...]) + sh_ref[...]).astype(o_ref.dtype)


def _normmod(x, g, scale2, shift2, *, tmod, n_prompt_tiles, out_dtype):
    m, d = x.shape
    mod_map = lambda i: (jnp.where(i >= n_prompt_tiles, 1, 0), 0)
    return pl.pallas_call(
        functools.partial(_normmod_body, eps=CFG["eps"]),
        grid=(m // tmod,),
        in_specs=[pl.BlockSpec((tmod, d), lambda i: (i, 0)),
                  pl.BlockSpec((1, d), lambda i: (0, 0)),
                  pl.BlockSpec((tmod, d), mod_map),
                  pl.BlockSpec((tmod, d), mod_map)],
        out_specs=pl.BlockSpec((tmod, d), lambda i: (i, 0)),
        out_shape=jax.ShapeDtypeStruct((m, d), out_dtype),
        compiler_params=_params("parallel"),
        name="normmod",
    )(x, g, scale2, shift2)


def _ep_bias(acc, j, bias):
    return acc + bias


def _group_rmsnorm(x, g, width, eps, scale):
    outs = []
    for c in range(x.shape[1] // width):
        blk = x[:, c * width:(c + 1) * width]
        ms = jnp.mean(blk * blk, axis=-1, keepdims=True)
        outs.append(blk * lax.rsqrt(ms + eps) * g[:, c * width:(c + 1) * width] * scale)
    return outs[0] if len(outs) == 1 else jnp.concatenate(outs, axis=1)


def _ep_qnorm(acc, j, g, *, hd, eps, scale):
    return _group_rmsnorm(acc, g, hd, eps, scale)


def _ep_kvnorm(acc, j, g, *, hd, eps, n_k_tiles):
    normed = _group_rmsnorm(acc, g, hd, eps, 1.0)
    return jnp.where(j < n_k_tiles, normed, acc)


def _ep_kw(acc, j, g, *, idx_dim, eps, wscale):
    lane = lax.broadcasted_iota(I32, acc.shape, 1)
    is_k = lane < idx_dim
    ms = jnp.sum(jnp.where(is_k, acc * acc, 0.0), axis=-1, keepdims=True) * (1.0 / idx_dim)
    return jnp.where(is_k, acc * lax.rsqrt(ms + eps) * g, acc * wscale)


def _ep_sigmoid(acc, j):
    return _sigmoid(acc)


def _sortable_keys(x):
    bits = lax.bitcast_convert_type(x, I32)
    return jnp.where(bits < 0, bits ^ 0x7FFFFFFF, bits)


def _head_scores(q2, w2, kc, rows, ih):
    s = lax.dot_general(q2, kc, (((1,), (1,)), ((), ())), preferred_element_type=F32)
    r = jnp.maximum(s, 0.0) * w2
    return jnp.sum(r.reshape(rows, ih, kc.shape[0]), axis=1)


def _kth_threshold(key_ref, nch, tk, ktop, rows):
    def count_ge(cand):
        def body(c, acc):
            off = pl.multiple_of(c * tk, tk)
            kk = key_ref[:, pl.ds(off, tk)]
            for s in range(tk // LANES):
                acc = acc + jnp.where(kk[:, s * LANES:(s + 1) * LANES] >= cand, 1, 0)
            return acc
        acc = lax.fori_loop(0, nch, body, jnp.zeros((rows, LANES), I32))
        return jnp.sum(acc, axis=-1, keepdims=True)

    def bit_step(b, t):
        cand = t + lax.shift_left(jnp.int32(1), 31 - b)
        return jnp.where(count_ge(cand) >= ktop, cand, t)

    return lax.fori_loop(0, 32, bit_step, jnp.full((rows, 1), INT_MIN, I32))


def _write_bias(key_ref, bias_store, thr, ktop, nch, nch_total, tk, rows, dtype):
    def count_gt(c, acc):
        kk = key_ref[:, pl.ds(pl.multiple_of(c * tk, tk), tk)]
        return acc + jnp.sum(jnp.where(kk > thr, 1.0, 0.0), axis=-1, keepdims=True)

    need = ktop - lax.fori_loop(0, nch, count_gt, jnp.zeros((rows, 1), F32))
    upper = (lax.broadcasted_iota(I32, (tk, tk), 0) <= lax.broadcasted_iota(I32, (tk, tk), 1))
    upper = jnp.where(upper, 1.0, 0.0).astype(BF16)

    def sel_chunk(c, seen):
        off = pl.multiple_of(c * tk, tk)
        kk = key_ref[:, pl.ds(off, tk)]
        tie = jnp.logical_and(kk == thr, kk > INT_MIN)
        tie_f = jnp.where(tie, 1.0, 0.0)
        prefix = seen + jnp.dot(tie_f.astype(BF16), upper, preferred_element_type=F32)
        sel = jnp.logical_or(kk > thr, jnp.logical_and(tie, prefix <= need))
        bias_store(off, jnp.where(sel, 0.0, NEG_BIG).astype(dtype))
        return seen + jnp.sum(tie_f, axis=-1, keepdims=True)

    def fill_chunk(c, _):
        off = pl.multiple_of(c * tk, tk)
        bias_store(off, jnp.full((rows, tk), NEG_BIG, dtype))
        return 0

    lax.fori_loop(0, nch, sel_chunk, jnp.zeros((rows, 1), F32))
    lax.fori_loop(nch, nch_total, fill_chunk, 0)


def _idx_prompt_body(q2_ref, w2_ref, kin_ref, bias_ref, key_ref, *, tq, tk, ih, ktop, s_len):
    t0 = pl.program_id(0) * tq
    nch = (t0 + tq + tk - 1) // tk
    q2 = q2_ref[...]
    w2 = w2_ref[...]
    row = t0 + lax.broadcasted_iota(I32, (tq, tk), 0)
    col = lax.broadcasted_iota(I32, (tq, tk), 1)

    def score_chunk(c, _):
        off = pl.multiple_of(c * tk, tk)
        sc = _head_scores(q2, w2, kin_ref[pl.ds(off, tk), :], tq, ih)
        key_ref[:, pl.ds(off, tk)] = jnp.where(col + off <= row, _sortable_keys(sc), INT_MIN)
        return 0

    lax.fori_loop(0, nch, score_chunk, 0)
    thr = _kth_threshold(key_ref, nch, tk, ktop, tq)

    def store(off, val):
        bias_ref[:, pl.ds(off, tk)] = val

    _write_bias(key_ref, store, thr, ktop, nch, s_len // tk, tk, tq, bias_ref.dtype)


def _idx_prompt(q2, w2, kin, *, s_len, ih, ktop, tq, tk):
    idim = q2.shape[1]
    body = functools.partial(_idx_prompt_body, tq=tq, tk=tk, ih=ih, ktop=ktop, s_len=s_len)
    return pl.pallas_call(
        body,
        grid=(s_len // tq,),
        in_specs=[pl.BlockSpec((tq * ih, idim), lambda i: (i, 0)),
                  pl.BlockSpec((tq * ih, 1), lambda i: (i, 0)),
                  pl.BlockSpec((s_len, idim), lambda i: (0, 0))],
        out_specs=pl.BlockSpec((tq, s_len), lambda i: (i, 0)),
        out_shape=jax.ShapeDtypeStruct((s_len, s_len), BF16),
        scratch_shapes=[pltpu.VMEM((tq, s_len), I32)],
        compiler_params=_params("parallel"),
        name="idx_prompt",
    )(q2, w2, kin)


PAGES_PER_STEP = 16


def _page_steps(n_pages):
    return -(-(n_pages + 1) // PAGES_PER_STEP)


def _page_specs(block, n_pages):
    def spec(r):
        return pl.BlockSpec(block, lambda b, p, pt: (pt[b, jnp.minimum(p * PAGES_PER_STEP + r, n_pages - 1)],)
                            + (0,) * (len(block) - 1))
    return [spec(r) for r in range(PAGES_PER_STEP)]


def _idx_sample_body(pt_ref, q2_ref, w2_ref, *rest, t, ih, ktop, n_pages, psz, n_steps):
    page_refs = rest[:PAGES_PER_STEP]
    knew_ref, bias_ref, key_ref = rest[PAGES_PER_STEP:]
    p = pl.program_id(1)
    q2 = q2_ref[0]
    w2 = w2_ref[0]
    row = lax.broadcasted_iota(I32, (t, psz), 0)
    col = lax.broadcasted_iota(I32, (t, psz), 1)
    for r in range(PAGES_PER_STEP):
        slot = p * PAGES_PER_STEP + r
        kb = jnp.where(slot < n_pages, page_refs[r][0], knew_ref[0]).astype(BF16)
        sc = _head_scores(q2, w2, kb, t, ih)
        valid = jnp.logical_or(slot < n_pages, jnp.logical_and(slot == n_pages, col <= row))
        key_ref[:, pl.ds(pl.multiple_of(slot * psz, psz), psz)] = jnp.where(valid, _sortable_keys(sc), INT_MIN)

    @pl.when(p == n_steps - 1)
    def _():
        nch = n_steps * PAGES_PER_STEP
        thr = _kth_threshold(key_ref, nch, psz, ktop, t)

        def store(off, val):
            bias_ref[0, :, pl.ds(off, psz)] = val

        _write_bias(key_ref, store, thr, ktop, nch, nch, psz, t, bias_ref.dtype)


def _idx_sample(page_table, q2, w2, pool_ki, knew, *, t, ih, ktop):
    bd, n_pages = page_table.shape
    psz, idim = pool_ki.shape[1], pool_ki.shape[2]
    n_steps = _page_steps(n_pages)
    width = n_steps * PAGES_PER_STEP * psz
    body = functools.partial(_idx_sample_body, t=t, ih=ih, ktop=ktop, n_pages=n_pages, psz=psz,
                             n_steps=n_steps)
    grid_spec = pltpu.PrefetchScalarGridSpec(
        num_scalar_prefetch=1,
        grid=(bd, n_steps),
        in_specs=[pl.BlockSpec((1, t * ih, idim), lambda b, p, pt: (b, 0, 0)),
                  pl.BlockSpec((1, t * ih, 1), lambda b, p, pt: (b, 0, 0))]
                 + _page_specs((1, psz, idim), n_pages)
                 + [pl.BlockSpec((1, psz, idim), lambda b, p, pt: (b, 0, 0))],
        out_specs=pl.BlockSpec((1, t, width), lambda b, p, pt: (b, 0, 0)),
        scratch_shapes=[pltpu.VMEM((t, width), I32)],
    )
    return pl.pallas_call(
        body, grid_spec=grid_spec,
        out_shape=jax.ShapeDtypeStruct((bd, t, width), F32),
        compiler_params=_params("parallel", "arbitrary"),
        name="idx_sample",
    )(page_table, q2, w2, *([pool_ki] * PAGES_PER_STEP), knew)


def _softmax_step(s, v, m_prev, l_prev, acc_prev):
    m_new = jnp.maximum(m_prev, jnp.max(s, axis=-1, keepdims=True))
    alpha = jnp.exp(m_prev - m_new)
    p = jnp.exp(s - m_new)
    l_new = alpha * l_prev + jnp.sum(p, axis=-1, keepdims=True)
    acc_new = alpha * acc_prev + jnp.dot(p.astype(BF16), v, preferred_element_type=F32)
    return m_new, l_new, acc_new


def _attn_prompt_body(q_ref, k_ref, v_ref, b_ref, o_ref, m_ref, l_ref, acc_ref, *, nh, kvh, hd, tq, tk):
    i = pl.program_id(0)
    j = pl.program_id(1)
    grp = nh // kvh
    j_last = (i * tq + tq - 1) // tk

    @pl.when(j == 0)
    def _():
        m_ref[...] = jnp.full(m_ref.shape, -jnp.inf, F32)
        l_ref[...] = jnp.zeros(l_ref.shape, F32)
        acc_ref[...] = jnp.zeros(acc_ref.shape, F32)

    @pl.when(j <= j_last)
    def _():
        bias = b_ref[...].astype(F32)
        for n in range(kvh):
            qn = jnp.concatenate([q_ref[:, (n * grp + g) * hd:(n * grp + g + 1) * hd] for g in range(grp)],
                                 axis=0)
            kn = k_ref[:, n * hd:(n + 1) * hd].astype(BF16)
            vn = v_ref[:, n * hd:(n + 1) * hd].astype(BF16)
            s = lax.dot_general(qn, kn, (((1,), (1,)), ((), ())), preferred_element_type=F32)
            s = (s.reshape(grp, tq, tk) + bias[None]).reshape(grp * tq, tk)
            m_ref[n], l_ref[n], acc_ref[n] = _softmax_step(s, vn, m_ref[n], l_ref[n], acc_ref[n])

    @pl.when(j == j_last)
    def _():
        for n in range(kvh):
            on = acc_ref[n] / l_ref[n]
            for g in range(grp):
                h = n * grp + g
                o_ref[:, h * hd:(h + 1) * hd] = on[g * tq:(g + 1) * tq].astype(o_ref.dtype)


def _attn_prompt(q, kv, bias, *, s_len, nh, kvh, hd, tq, tk):
    aw, kvw = nh * hd, kvh * hd
    grp = nh // kvh
    body = functools.partial(_attn_prompt_body, nh=nh, kvh=kvh, hd=hd, tq=tq, tk=tk)
    key_blk = lambda i, j: jnp.minimum(j, (i * tq + tq - 1) // tk)
    return pl.pallas_call(
        body,
        grid=(s_len // tq, s_len // tk),
        in_specs=[pl.BlockSpec((tq, aw), lambda i, j: (i, 0)),
                  pl.BlockSpec((tk, kvw), lambda i, j: (key_blk(i, j), 0)),
                  pl.BlockSpec((tk, kvw), lambda i, j: (key_blk(i, j), 1)),
                  pl.BlockSpec((tq, tk), lambda i, j: (i, key_blk(i, j)))],
        out_specs=pl.BlockSpec((tq, aw), lambda i, j: (i, 0)),
        out_shape=jax.ShapeDtypeStruct((s_len, aw), BF16),
        scratch_shapes=[pltpu.VMEM((kvh, grp * tq, 1), F32), pltpu.VMEM((kvh, grp * tq, 1), F32),
                        pltpu.VMEM((kvh, grp * tq, hd), F32)],
        compiler_params=_params("parallel", "arbitrary"),
        name="attn_prompt",
    )(q, kv, kv, bias)


def _attn_sample_body(pt_ref, q_ref, *rest, t, nh, kvh, hd, n_pages, n_steps):
    kp_refs = rest[:PAGES_PER_STEP]
    vp_refs = rest[PAGES_PER_STEP:2 * PAGES_PER_STEP]
    kn_ref, vn_ref, b_ref, o_ref, m_ref, l_ref, acc_ref = rest[2 * PAGES_PER_STEP:]
    p = pl.program_id(1)
    grp = nh // kvh

    @pl.when(p == 0)
    def _():
        m_ref[...] = jnp.full(m_ref.shape, -jnp.inf, F32)
        l_ref[...] = jnp.zeros(l_ref.shape, F32)
        acc_ref[...] = jnp.zeros(acc_ref.shape, F32)

    bias = jnp.concatenate([b_ref[0]] * grp, axis=0)
    q = q_ref[0]
    for n in range(kvh):
        qn = jnp.concatenate([q[:, (n * grp + g) * hd:(n * grp + g + 1) * hd] for g in range(grp)],
                             axis=0).astype(BF16)
        ks, vs = [], []
        for r in range(PAGES_PER_STEP):
            past = p * PAGES_PER_STEP + r < n_pages
            ks.append(jnp.where(past, kp_refs[r][0, :, n, :], kn_ref[0, :, n * hd:(n + 1) * hd]))
            vs.append(jnp.where(past, vp_refs[r][0, :, n, :], vn_ref[0, :, n * hd:(n + 1) * hd]))
        kn = jnp.concatenate(ks, axis=0).astype(BF16)
        vn = jnp.concatenate(vs, axis=0).astype(BF16)
        s = lax.dot_general(qn, kn, (((1,), (1,)), ((), ())), preferred_element_type=F32) + bias
        m_ref[n], l_ref[n], acc_ref[n] = _softmax_step(s, vn, m_ref[n], l_ref[n], acc_ref[n])

    @pl.when(p == n_steps - 1)
    def _():
        for n in range(kvh):
            on = acc_ref[n] / l_ref[n]
            for g in range(grp):
                h = n * grp + g
                o_ref[0, :, h * hd:(h + 1) * hd] = on[g * t:(g + 1) * t].astype(o_ref.dtype)


def _attn_sample(page_table, q, pool_k, pool_v, knew, vnew, bias, *, t, nh, kvh, hd):
    bd, n_pages = page_table.shape
    psz = pool_k.shape[1]
    aw, kvw = nh * hd, kvh * hd
    grp = nh // kvh
    n_steps = _page_steps(n_pages)
    body = functools.partial(_attn_sample_body, t=t, nh=nh, kvh=kvh, hd=hd, n_pages=n_pages,
                             n_steps=n_steps)
    own_map = lambda b, p, pt: (b, 0, 0)
    page_specs = _page_specs((1, psz, kvh, hd), n_pages)
    grid_spec = pltpu.PrefetchScalarGridSpec(
        num_scalar_prefetch=1,
        grid=(bd, n_steps),
        in_specs=[pl.BlockSpec((1, t, aw), own_map)] + page_specs + page_specs
                 + [pl.BlockSpec((1, psz, kvw), own_map),
                    pl.BlockSpec((1, psz, kvw), own_map),
                    pl.BlockSpec((1, t, PAGES_PER_STEP * psz), lambda b, p, pt: (b, 0, p))],
        out_specs=pl.BlockSpec((1, t, aw), own_map),
        scratch_shapes=[pltpu.VMEM((kvh, grp * t, 1), F32), pltpu.VMEM((kvh, grp * t, 1), F32),
                        pltpu.VMEM((kvh, grp * t, hd), F32)],
    )
    return pl.pallas_call(
        body, grid_spec=grid_spec,
        out_shape=jax.ShapeDtypeStruct((bd, t, aw), F32),
        compiler_params=_params("parallel", "arbitrary"),
        name="attn_sample",
    )(page_table, q, *([pool_k] * PAGES_PER_STEP), *([pool_v] * PAGES_PER_STEP), knew, vnew, bias)


def _softplus(x):
    return jnp.maximum(x, 0.0) + jnp.log(1.0 + jnp.exp(-jnp.abs(x)))


def _dn_prep_body(u_ref, prev_ref, hist_ref, cw_ref, ba_ref, al_ref, dtb_ref, y_ref, bg_ref,
                  *, tm, tiles_per_seq, nh, hd, conv_w, eps):
    first = (pl.program_id(0) % tiles_per_seq) == 0
    r8 = lax.broadcasted_iota(I32, (SUBLANES, hd), 0)
    qscale = hd ** -0.5
    for c in range(u_ref.shape[1] // hd):
        cs = slice(c * hd, (c + 1) * hd)
        cur = u_ref[:, cs]
        prev = jnp.where(first, hist_ref[0, :, cs], prev_ref[:, cs])
        y = cur * cw_ref[conv_w - 1:conv_w, cs]
        for k in range(1, conv_w):
            sk = pltpu.roll(cur, k, 0)
            top = jnp.where(r8 < k, pltpu.roll(prev, k, 0), sk[:SUBLANES])
            sk = top if tm == SUBLANES else jnp.concatenate([top, sk[SUBLANES:]], axis=0)
            y = y + sk * cw_ref[conv_w - 1 - k:conv_w - k, cs]
        y = _silu(y)
        if c < 2 * nh:
            y = y * lax.rsqrt(jnp.sum(y * y, axis=-1, keepdims=True) + eps)
            if c < nh:
                y = y * qscale
        y_ref[:, cs] = y
    ba = ba_ref[...]
    lane = lax.broadcasted_iota(I32, ba.shape, 1)
    decay = -jnp.exp(al_ref[...]) * _softplus(ba + dtb_ref[...])
    bg_ref[...] = jnp.where(lane < nh, _sigmoid(ba), jnp.where(lane < 2 * nh, decay, 0.0))


def _dn_prep(u, hist8, cw8, ba, al_row, dtb_row, *, row_off, nrows, tm, seq_len, nh, hd):
    cch = u.shape[1]
    tps = seq_len // tm
    body = functools.partial(_dn_prep_body, tm=tm, tiles_per_seq=tps, nh=nh, hd=hd,
                             conv_w=CFG["conv_w"], eps=CFG["eps"])
    ob, o8 = row_off // tm, row_off // SUBLANES
    return pl.pallas_call(
        body,
        grid=(nrows // tm,),
        in_specs=[pl.BlockSpec((tm, cch), lambda i: (ob + i, 0)),
                  pl.BlockSpec((SUBLANES, cch),
                               lambda i: (jnp.maximum(o8 + i * (tm // SUBLANES) - 1, 0), 0)),
                  pl.BlockSpec((1, SUBLANES, cch), lambda i: (i // tps, 0, 0)),
                  pl.BlockSpec((SUBLANES, cch), lambda i: (0, 0)),
                  pl.BlockSpec((tm, LANES), lambda i: (ob + i, 0)),
                  pl.BlockSpec((1, LANES), lambda i: (0, 0)),
                  pl.BlockSpec((1, LANES), lambda i: (0, 0))],
        out_specs=[pl.BlockSpec((tm, cch), lambda i: (i, 0)),
                   pl.BlockSpec((tm, LANES), lambda i: (i, 0))],
        out_shape=[jax.ShapeDtypeStruct((nrows, cch), F32),
                   jax.ShapeDtypeStruct((nrows, LANES), F32)],
        compiler_params=_params("parallel"),
        name="dn_prep",
    )(u, u, hist8, cw8, ba, al_row, dtb_row)


def _dot3(a, b, dims):
    a_hi = a.astype(BF16)
    b_hi = b.astype(BF16)
    a_lo = (a - a_hi.astype(F32)).astype(BF16)
    b_lo = (b - b_hi.astype(F32)).astype(BF16)
    mm = lambda x, y: lax.dot_general(x, y, (dims, ((), ())), preferred_element_type=F32)
    return mm(a_hi, b_hi) + (mm(a_lo, b_hi) + mm(a_hi, b_lo))


def _bdot3(a, b, contract):
    a_hi = a.astype(BF16)
    b_hi = b.astype(BF16)
    a_lo = (a - a_hi.astype(F32)).astype(BF16)
    b_lo = (b - b_hi.astype(F32)).astype(BF16)
    mm = lambda x, y: lax.dot_general(x, y, (contract, ((0,), (0,))), preferred_element_type=F32)
    return mm(a_hi, b_hi) + (mm(a_lo, b_hi) + mm(a_hi, b_lo))


def _dot_hi(a, b):
    return _dot3(a, b, ((1,), (0,)))


def _dot_nt_hi(a, b):
    return _dot3(a, b, ((1,), (1,)))


def _dot_tn_hi(a, b):
    return _dot3(a, b, ((0,), (0,)))


def _dn_scan_body(y_ref, bg_ref, z_ref, s0_ref, ng_ref, o_ref, s_ref, *, c, nh, hd, eps):
    @pl.when(pl.program_id(1) == 0)
    def _():
        s_ref[...] = s0_ref[...]

    bg = bg_ref[...]
    ri = lax.broadcasted_iota(I32, (c, c), 0)
    ci = lax.broadcasted_iota(I32, (c, c), 1)
    causal = ri >= ci
    strict = ri > ci
    eye = jnp.where(ri == ci, 1.0, 0.0)
    gcum = jnp.dot(jnp.where(causal, 1.0, 0.0), bg, precision=HIGHEST,
                   preferred_element_type=F32)
    gpad = gcum if c == LANES else jnp.concatenate([gcum, jnp.zeros((LANES - c, LANES), F32)], axis=0)
    g_t = gpad.T
    n_double = int(math.log2(c)) - 1
    heads = lambda off: jnp.stack([y_ref[:, (off + h) * hd:(off + h + 1) * hd] for h in range(nh)])
    cols = lambda arr, off: jnp.stack([arr[:, off + h:off + h + 1] for h in range(nh)])
    q, k, v = heads(0), heads(nh), heads(2 * nh)
    beta, gcol = cols(bg, 0), cols(gcum, nh)
    grow = jnp.stack([g_t[nh + h:nh + h + 1, :c] for h in range(nh)])
    glast = gcol[:, c - 1:c, :]
    decay = jnp.where(causal[None], jnp.exp(jnp.where(causal[None], gcol - grow, 0.0)), 0.0)
    kq_k = _bdot3(jnp.concatenate([k, q], axis=1), k, ((2,), (2,)))
    a = jnp.where(strict[None], beta * kq_k[:, :c] * decay, 0.0)
    p = -a
    tinv = eye[None] + p
    for _ in range(n_double):
        p = _bdot3(p, p, ((2,), (1,)))
        tinv = tinv + _bdot3(tinv, p, ((2,), (1,)))
    eg = jnp.exp(gcol)
    uw = _bdot3(tinv, jnp.concatenate([beta * v, (beta * eg) * k], axis=2), ((2,), (1,)))
    u, w = uw[:, :, :hd], uw[:, :, hd:]
    qk = kq_k[:, c:] * decay
    s = s_ref[0]
    wq_s = _bdot3(jnp.concatenate([w, q * eg], axis=1), s, ((2,), (1,)))
    e = u - wq_s[:, :c]
    o = wq_s[:, c:] + _bdot3(qk, e, ((2,), (1,)))
    s_ref[0] = s * jnp.exp(glast) + _bdot3(k * jnp.exp(glast - gcol), e, ((1,), (1,)))
    on = o * lax.rsqrt(jnp.mean(o * o, axis=-1, keepdims=True) + eps) * ng_ref[...]
    for h in range(nh):
        o_ref[:, h * hd:(h + 1) * hd] = (on[h] * _silu(z_ref[:, h * hd:(h + 1) * hd])).astype(o_ref.dtype)


def _dn_scan(y, bg, z, s0, ng_row, *, z_row_off, c, seq_len, nh, hd):
    nseq = s0.shape[0]
    nch = seq_len // c
    zo = z_row_off // c
    body = functools.partial(_dn_scan_body, c=c, nh=nh, hd=hd, eps=CFG["eps"])
    return pl.pallas_call(
        body,
        grid=(nseq, nch),
        in_specs=[pl.BlockSpec((c, y.shape[1]), lambda b, n: (b * nch + n, 0)),
                  pl.BlockSpec((c, LANES), lambda b, n: (b * nch + n, 0)),
                  pl.BlockSpec((c, nh * hd), lambda b, n: (zo + b * nch + n, 0)),
                  pl.BlockSpec((1, nh, hd, hd), lambda b, n: (b, 0, 0, 0)),
                  pl.BlockSpec((1, hd), lambda b, n: (0, 0))],
        out_specs=[pl.BlockSpec((c, nh * hd), lambda b, n: (b * nch + n, 0)),
                   pl.BlockSpec((1, nh, hd, hd), lambda b, n: (b, 0, 0, 0))],
        out_shape=[jax.ShapeDtypeStruct((nseq * seq_len, nh * hd), F32),
                   jax.ShapeDtypeStruct(s0.shape, F32)],
        compiler_params=_params("parallel", "arbitrary"),
        name="dn_scan",
    )(y, bg, z, s0, ng_row)


def _first_argmax(x, lane, width):
    m = jnp.max(x, axis=-1, keepdims=True)
    return jnp.min(jnp.where(x == m, lane, width), axis=-1, keepdims=True)


def _router_body(h_ref, wr_ref, rb_ref, te_ref, tr_ref, tw_ref, rkt_ref, cnt_ref,
                 *, tm, ne, n_groups, topk_groups, top_k, scale):
    @pl.when(pl.program_id(0) == 0)
    def _():
        cnt_ref[...] = jnp.zeros(cnt_ref.shape, F32)

    logits = jnp.dot(h_ref[...].astype(BF16), wr_ref[...], preferred_element_type=F32)
    scores = _sigmoid(logits)
    biased = scores + rb_ref[...]
    lane = lax.broadcasted_iota(I32, (tm, ne), 1)
    lane_grp = lax.shift_right_logical(lane, int(math.log2(ne // n_groups)))
    ninf = -jnp.inf
    gscore = jnp.full((tm, ne), ninf, F32)
    for g in range(n_groups):
        xg = jnp.where(lane_grp == g, biased, ninf)
        m1 = jnp.max(xg, axis=-1, keepdims=True)
        i1 = jnp.min(jnp.where(xg == m1, lane, ne), axis=-1, keepdims=True)
        m2 = jnp.max(jnp.where(lane == i1, ninf, xg), axis=-1, keepdims=True)
        gscore = jnp.where(lane == g, m1 + m2, gscore)
    gsel = jnp.zeros((tm, ne), F32)
    for _ in range(topk_groups):
        ig = _first_argmax(gscore, lane, ne)
        gsel = jnp.where(lane_grp == ig, 1.0, gsel)
        gscore = jnp.where(lane == ig, ninf, gscore)
    x = jnp.where(gsel > 0.0, biased, ninf)
    sel = jnp.zeros((tm, ne), F32)
    idxs, ws = [], []
    for _ in range(top_k):
        ie = _first_argmax(x, lane, ne)
        hit = lane == ie
        idxs.append(ie)
        ws.append(jnp.sum(jnp.where(hit, scores, 0.0), axis=-1, keepdims=True))
        sel = jnp.where(hit, 1.0, sel)
        x = jnp.where(hit, ninf, x)
    wsum = functools.reduce(lambda p, q: p + q, ws)
    r = lax.broadcasted_iota(I32, (tm, tm), 0)
    cc = lax.broadcasted_iota(I32, (tm, tm), 1)
    earlier = jnp.where(r > cc, 1.0, 0.0).astype(BF16)
    rank = cnt_ref[...] + jnp.dot(earlier, sel.astype(BF16), preferred_element_type=F32)
    cnt_ref[...] += jnp.sum(sel, axis=0, keepdims=True)
    te = jnp.zeros((tm, ne), I32)
    tr = jnp.zeros((tm, ne), F32)
    tw = jnp.zeros((tm, ne), F32)
    for kk in range(top_k):
        slot = lane == kk
        te = jnp.where(slot, idxs[kk], te)
        tr = jnp.where(slot, jnp.sum(jnp.where(lane == idxs[kk], rank, 0.0), axis=-1, keepdims=True), tr)
        tw = jnp.where(slot, ws[kk] / wsum * scale, tw)
    te_ref[...] = te
    tr_ref[...] = tr
    tw_ref[...] = tw
    rkt_ref[...] = jnp.where(sel > 0.0, rank, -1.0).T


def _router(h2, w_router, rb_row, *, tm):
    m, d = h2.shape
    ne = w_router.shape[1]
    assert ne == LANES
    body = functools.partial(_router_body, tm=tm, ne=ne, n_groups=CFG["n_groups"],
                             topk_groups=CFG["topk_groups"], top_k=CFG["top_k"],
                             scale=CFG["routed_scale"])
    tile = pl.BlockSpec((tm, ne), lambda i: (i, 0))
    return pl.pallas_call(
        body,
        grid=(m // tm,),
        in_specs=[pl.BlockSpec((tm, d), lambda i: (i, 0)),
                  pl.BlockSpec((d, ne), lambda i: (0, 0)),
                  pl.BlockSpec((1, ne), lambda i: (0, 0))],
        out_specs=[tile, tile, tile, pl.BlockSpec((ne, tm), lambda i: (0, i)),
                   pl.BlockSpec((1, ne), lambda i: (0, 0))],
        out_shape=[jax.ShapeDtypeStruct((m, ne), I32), jax.ShapeDtypeStruct((m, ne), F32),
                   jax.ShapeDtypeStruct((m, ne), F32), jax.ShapeDtypeStruct((ne, m), F32),
                   jax.ShapeDtypeStruct((1, ne), F32)],
        compiler_params=_params("arbitrary"),
        name="router",
    )(h2, w_router, rb_row)


def _dest_body(te_ref, tr_ref, ps_ref, d_ref, *, top_k):
    te = te_ref[...]
    tr = tr_ref[...]
    lane = lax.broadcasted_iota(I32, te.shape, 1)
    d = jnp.zeros(te.shape, F32)
    for kk in range(top_k):
        base = jnp.sum(jnp.where(lane == te[:, kk:kk + 1], ps_ref[...], 0.0), axis=-1, keepdims=True)
        d = jnp.where(lane == kk, base + tr[:, kk:kk + 1], d)
    d_ref[...] = d.astype(I32)


def _dest(te, tr, pad_start_row, *, tm):
    m, ne = te.shape
    tile = pl.BlockSpec((tm, ne), lambda i: (i, 0))
    return pl.pallas_call(
        functools.partial(_dest_body, top_k=CFG["top_k"]),
        grid=(m // tm,),
        in_specs=[tile, tile, pl.BlockSpec((1, ne), lambda i: (0, 0))],
        out_specs=tile,
        out_shape=jax.ShapeDtypeStruct((m, ne), I32),
        compiler_params=_params("parallel"),
        name="dest",
    )(te, tr, pad_start_row)


def _row_tokens_body(sbe_ref, off_ref, rk_ref, tok_ref, *, sb, sub):
    rk = rk_ref[0]
    tok = lax.broadcasted_iota(I32, rk.shape, 1).astype(F32)
    base = off_ref[pl.program_id(0)]
    for r0 in range(0, sb, sub):
        target = (base + r0 + lax.broadcasted_iota(I32, (sub, 1), 0)).astype(F32)
        tok_ref[r0:r0 + sub, :] = jnp.sum(jnp.where(rk == target, tok, 0.0), axis=-1,
                                          keepdims=True).astype(I32)


def _row_tokens(sb_e, sb_off, rkt, *, sb):
    ne, m = rkt.shape
    n_sb = sb_e.shape[0]
    grid_spec = pltpu.PrefetchScalarGridSpec(
        num_scalar_prefetch=2,
        grid=(n_sb,),
        in_specs=[pl.BlockSpec((1, 1, m), lambda s, sbe, off: (sbe[s], 0, 0))],
        out_specs=pl.BlockSpec((sb, 1), lambda s, sbe, off: (s, 0)),
    )
    return pl.pallas_call(
        functools.partial(_row_tokens_body, sb=sb, sub=min(sb, LANES)), grid_spec=grid_spec,
        out_shape=jax.ShapeDtypeStruct((n_sb * sb, 1), I32),
        compiler_params=_params("parallel"),
        name="moe_row_tokens",
    )(sb_e, sb_off, rkt.reshape(ne, 1, m))


N_WCHUNK = 16


def _gather_rows(ids_ref, src_hbm, dst, sem, n):
    def body(g, _):
        for j in range(SUBLANES):
            pltpu.make_async_copy(src_hbm.at[pl.ds(ids_ref[g * SUBLANES + j], 1)],
                                  dst.at[g, pl.ds(j, 1)], sem).start()
        return 0
    lax.fori_loop(0, n // SUBLANES, body, 0)


def _wait_rows(src_hbm, dst, sem, n):
    def body(g, _):
        for j in range(SUBLANES):
            pltpu.make_async_copy(src_hbm.at[pl.ds(0, 1)], dst.at[g, pl.ds(j, 1)], sem).wait()
        return 0
    lax.fori_loop(0, n // SUBLANES, body, 0)


def _experts_body(sbe_ref, nu_ref, tok_ref, tok_next_ref, x_hbm, wg_hbm, wu_hbm, wd_hbm, y_ref,
                  xstage, xb16, wgu, wdn, stg_g, stg_u, stg_d, gacc, sem_x, sem_w, *, sb, d, de):
    s = pl.program_id(0)
    nu = nu_ref[0]
    used = s < nu
    e = sbe_ref[s]
    new_e = jnp.logical_or(s == 0, e != sbe_ref[jnp.maximum(s - 1, 0)])
    kc, hc = d // N_WCHUNK, de // N_WCHUNK

    def chunk_copies(ex, c, slot):
        return (pltpu.make_async_copy(wg_hbm.at[ex, pl.ds(c * kc, kc), :], stg_g.at[slot], sem_w.at[slot]),
                pltpu.make_async_copy(wu_hbm.at[ex, pl.ds(c * kc, kc), :], stg_u.at[slot], sem_w.at[slot]),
                pltpu.make_async_copy(wd_hbm.at[ex, pl.ds(c * hc, hc), :], stg_d.at[slot], sem_w.at[slot]))

    def start_chunk(ex, c):
        for cp in chunk_copies(ex, c, c % 2):
            cp.start()

    @pl.when(used)
    def _():
        @pl.when(s == 0)
        def _():
            _gather_rows(tok_ref, x_hbm, xstage, sem_x, sb)
            start_chunk(e, 0)
            start_chunk(e, 1)

        _wait_rows(x_hbm, xstage, sem_x, sb)
        xb16[...] = xstage[...].reshape(sb, d).astype(BF16)

        @pl.when(s + 1 < nu)
        def _():
            _gather_rows(tok_next_ref, x_hbm, xstage, sem_x, sb)

    @pl.when(jnp.logical_and(used, new_e))
    def _():
        for c in range(N_WCHUNK):
            for cp in chunk_copies(e, c, c % 2):
                cp.wait()
            rows = slice(c * kc, (c + 1) * kc)
            wgu[rows, :de] = stg_g[c % 2].astype(BF16)
            wgu[rows, de:] = stg_u[c % 2].astype(BF16)
            wdn[c * hc:(c + 1) * hc, :] = stg_d[c % 2].astype(BF16)
            if c + 2 < N_WCHUNK:
                start_chunk(e, c + 2)
            part = jnp.dot(xb16[:, rows], wgu[rows, :], preferred_element_type=F32)
            if c == 0:
                gacc[...] = part
            else:
                gacc[...] += part

    @pl.when(jnp.logical_and(used, jnp.logical_not(new_e)))
    def _():
        gacc[...] = jnp.dot(xb16[...], wgu[...], preferred_element_type=F32)

    @pl.when(used)
    def _():
        nxt = sbe_ref[jnp.minimum(s + 1, sbe_ref.shape[0] - 1)]

        @pl.when(jnp.logical_and(s + 1 < nu, nxt != e))
        def _():
            start_chunk(nxt, 0)
            start_chunk(nxt, 1)

        hmid = (_silu(gacc[:, :de]) * gacc[:, de:]).astype(BF16)
        y_ref[...] = jnp.dot(hmid, wdn[...], preferred_element_type=F32)

    @pl.when(jnp.logical_not(used))
    def _():
        y_ref[...] = jnp.zeros(y_ref.shape, F32)


def _experts(sb_e, n_used, row_tok, x, w_gate, w_up, w_down, *, sb):
    d = x.shape[1]
    de = w_gate.shape[2]
    n_sb = sb_e.shape[0]
    kc, hc = d // N_WCHUNK, de // N_WCHUNK

    def s_eff(s, nu):
        return jnp.minimum(s, nu[0] - 1)

    grid_spec = pltpu.PrefetchScalarGridSpec(
        num_scalar_prefetch=2,
        grid=(n_sb,),
        in_specs=[pl.BlockSpec((sb,), lambda s, sbe, nu: (s_eff(s, nu),), memory_space=pltpu.SMEM),
                  pl.BlockSpec((sb,), lambda s, sbe, nu: (s_eff(s + 1, nu),), memory_space=pltpu.SMEM),
                  pl.BlockSpec(memory_space=pl.ANY), pl.BlockSpec(memory_space=pl.ANY),
                  pl.BlockSpec(memory_space=pl.ANY), pl.BlockSpec(memory_space=pl.ANY)],
        out_specs=pl.BlockSpec((sb, d), lambda s, sbe, nu: (s, 0)),
        scratch_shapes=[pltpu.VMEM((sb // SUBLANES, SUBLANES, d), F32), pltpu.VMEM((sb, d), BF16),
                        pltpu.VMEM((d, 2 * de), BF16), pltpu.VMEM((de, d), BF16),
                        pltpu.VMEM((2, kc, de), F32), pltpu.VMEM((2, kc, de), F32),
                        pltpu.VMEM((2, hc, d), F32), pltpu.VMEM((sb, 2 * de), F32),
                        pltpu.SemaphoreType.DMA(()), pltpu.SemaphoreType.DMA((2,))],
    )
    return pl.pallas_call(
        functools.partial(_experts_body, sb=sb, d=d, de=de), grid_spec=grid_spec,
        out_shape=jax.ShapeDtypeStruct((n_sb * sb, d), F32),
        compiler_params=_params("arbitrary"),
        name="moe_experts",
    )(sb_e, n_used, row_tok, row_tok, x, w_gate, w_up, w_down)


def _combine_body(dest_ref, dest_next_ref, ys_hbm, w_ref, sh_ref, x1_ref, g2_ref, o_ref, buf, sem,
                  *, tc, top_k, n_steps):
    i = pl.program_id(0)
    slot = lax.rem(i, 2)
    n = tc * top_k

    def gather(ids_ref, sl):
        def body(t, _):
            for k in range(top_k):
                pltpu.make_async_copy(ys_hbm.at[pl.ds(ids_ref[t * top_k + k], 1)],
                                      buf.at[sl, t, pl.ds(k, 1)], sem.at[sl]).start()
            return 0
        lax.fori_loop(0, tc, body, 0)

    @pl.when(i == 0)
    def _():
        gather(dest_ref, 0)

    @pl.when(i + 1 < n_steps)
    def _():
        gather(dest_next_ref, 1 - slot)

    def wait(t, _):
        for k in range(top_k):
            pltpu.make_async_copy(ys_hbm.at[pl.ds(0, 1)], buf.at[slot, t, pl.ds(k, 1)],
                                  sem.at[slot]).wait()
        return 0

    lax.fori_loop(0, tc, wait, 0)
    routed = jnp.sum(buf[slot] * w_ref[...], axis=1)
    o_ref[...] = x1_ref[...] + g2_ref[...] * (routed + sh_ref[...])


def _combine(dest_flat, ys, w3, shared, x1, gate2, *, tc, tmod, n_prompt_tiles):
    top_k = CFG["top_k"]
    m, d = x1.shape
    n_steps = m // tc
    per = tmod // tc
    npt = n_prompt_tiles * per

    def g2_map(i):
        return (jnp.where(i >= npt, per, 0) + i % per, 0)

    row = pl.BlockSpec((tc, d), lambda i: (i, 0))
    return pl.pallas_call(
        functools.partial(_combine_body, tc=tc, top_k=top_k, n_steps=n_steps),
        grid=(n_steps,),
        in_specs=[pl.BlockSpec((tc * top_k,), lambda i: (i,), memory_space=pltpu.SMEM),
                  pl.BlockSpec((tc * top_k,), lambda i: (jnp.minimum(i + 1, n_steps - 1),),
                               memory_space=pltpu.SMEM),
                  pl.BlockSpec(memory_space=pl.ANY),
                  pl.BlockSpec((tc, top_k, 1), lambda i: (i, 0, 0)),
                  row, row, pl.BlockSpec((tc, d), g2_map)],
        out_specs=row,
        out_shape=jax.ShapeDtypeStruct((m, d), F32),
        scratch_shapes=[pltpu.VMEM((2, tc, top_k, d), F32), pltpu.SemaphoreType.DMA((2,))],
        compiler_params=_params("arbitrary"),
        name="moe_combine",
    )(dest_flat, dest_flat, ys, w3, shared, x1, gate2)


SUPER_BLOCK = 256
ROW_TILES = (768, 512, 256, 128, 64, 32, 16)
COL_TILES = (512, 256, 128)


def _pad_rows(a, n, axis, front=False):
    pad = [(0, 0)] * a.ndim
    pad[axis] = (n - a.shape[axis], 0) if front else (0, n - a.shape[axis])
    return jnp.pad(a, pad)


def _lane_row(vec, offset):
    return jnp.zeros((1, LANES), F32).at[0, offset:offset + vec.shape[0]].set(vec.astype(F32))


def _moe(h2, x1, gate2, lw, *, tmod, npt, tmm):
    m, d = h2.shape
    top_k = CFG["top_k"]
    ne = lw["w_router"].shape[1]
    te, tr, tw, rkt, counts = _router(h2, lw["w_router"].astype(BF16),
                                      lw["router_bias"][None].astype(F32), tm=tmod)
    sb = SUPER_BLOCK
    n_sb = -(-(m * top_k + ne * (sb - 1)) // sb)
    cnt = counts[0].astype(I32)
    padded = (cnt + sb - 1) // sb * sb
    pad_end = jnp.cumsum(padded)
    pad_start = pad_end - padded
    n_used = (pad_end[-1:] // sb).astype(I32)
    sb_row0 = jnp.arange(n_sb, dtype=I32) * sb
    sb_e = jnp.minimum(jnp.sum((pad_end[None, :] <= sb_row0[:, None]).astype(I32), axis=1), ne - 1)
    dest = _dest(te, tr, pad_start.astype(F32)[None], tm=tmod)[:, :top_k].reshape(-1)
    row_tok = _row_tokens(sb_e, sb_row0 - pad_start[sb_e], rkt, sb=sb).reshape(-1)
    ys = _experts(sb_e, n_used, row_tok, h2, lw["w_e_gate"], lw["w_e_up"], lw["w_e_down"], sb=sb)
    tns = _pick(lw["w_s_gate"].shape[1], COL_TILES)
    gs = _mm([h2], lw["w_s_gate"].astype(BF16), tm=tmm, tn=tns, name="shared_gate")
    us = _mm([h2], lw["w_s_up"].astype(BF16), tm=tmm, tn=tns, name="shared_up")
    shared = _mm([gs, us], lw["w_s_down"].astype(BF16), tm=tmm, tn=_pick(d, COL_TILES),
                 prologue=lambda g, u: _silu(g) * u, name="shared_down")
    return _combine(dest, ys, tw[:, :top_k].reshape(m, top_k, 1), shared, x1, gate2,
                    tc=min(32, tmod), tmod=tmod, n_prompt_tiles=npt)


def _layer(xp, xs, cp, cs, pool_k, pool_v, pool_ki, st_conv, st_delta, page_table, lw):
    eps = CFG["eps"]
    bp, s_len, d = xp.shape
    bd, t, _ = xs.shape
    assert bp == 1 and t % SUBLANES == 0 and t >= CFG["conv_w"] - 1
    n_s = bd * t
    m = s_len + n_s
    tmod = n_s
    assert s_len % tmod == 0
    npt = s_len // tmod
    tmm = _pick(m, ROW_TILES)
    hd = lw["q_norm_g"].shape[0]
    aw = lw["w_br_a"].shape[0]
    nh = aw // hd
    n_phys, psz, kvh, _ = pool_k.shape
    kvw = kvh * hd
    idim = pool_ki.shape[2]
    dnh = lw["dn_a_log"].shape[0]
    dv = lw["dn_norm_g"].shape[0]
    dnw = lw["w_br_b"].shape[0]
    cch = lw["conv_w"].shape[1]
    assert cch == 3 * dnw and dnw == dnh * dv
    n_in = lw["w_in"].shape[1]
    ih = (n_in - (aw + 2 * kvw + idim + cch + dnw + 2 * dnh + 2 * d)) // (idim + 1)
    assert idim + ih <= LANES and 2 * dnh <= LANES

    n_c = bp + bd
    c_all = _pad_rows(jnp.concatenate([cp, cs], axis=0), -(-n_c // 16) * 16, 0)
    mod = _mm([c_all], lw["w_ada"], tm=c_all.shape[0], tn=_pick(6 * d, COL_TILES),
              prologue=_silu, epilogue=_ep_bias, extras=[lw["b_ada"][None]], name="ada")

    def expand(a):
        return jnp.concatenate([jnp.broadcast_to(a[0:1], (tmod, d)),
                                jnp.repeat(a[bp:bp + bd], t, axis=0)], axis=0)

    shift1, scale1, gate1, shift2, scale2, gate2 = (expand(a) for a in jnp.split(mod, 6, axis=1))
    x_all = jnp.concatenate([xp.reshape(s_len, d), xs.reshape(n_s, d)], axis=0)
    h = _normmod(x_all, lw["norm1_g"][None], scale1, shift1, tmod=tmod, n_prompt_tiles=npt,
                 out_dtype=BF16)

    offs = [0]
    for wdt in (aw, kvw, kvw, ih * idim, idim, ih, cch, dnw, dnh, dnh, d, d):
        offs.append(offs[-1] + wdt)
    w_in = lw["w_in"]

    def cols(a, b, pad_to=None):
        w = w_in[:, offs[a]:offs[b]].astype(BF16)
        return w if pad_to is None else _pad_rows(w, pad_to, 1)

    qn = _mm([h], cols(0, 1), tm=tmm, tn=_pick(aw, COL_TILES), out_dtype=BF16, name="proj_q",
             epilogue=functools.partial(_ep_qnorm, hd=hd, eps=eps, scale=hd ** -0.5),
             extras=[jnp.tile(lw["q_norm_g"], nh)[None]])
    tkv = _pick(kvw, COL_TILES)
    kv = _mm([h], cols(1, 3), tm=tmm, tn=tkv, name="proj_kv",
             epilogue=functools.partial(_ep_kvnorm, hd=hd, eps=eps, n_k_tiles=kvw // tkv),
             extras=[jnp.concatenate([jnp.tile(lw["k_norm_g"], kvh), jnp.ones((kvw,), F32)])[None]])
    qi = _mm([h], cols(3, 4), tm=tmm, tn=_pick(ih * idim, COL_TILES), out_dtype=BF16, name="proj_qi")
    kw = _mm([h], cols(4, 6, LANES), tm=tmm, tn=LANES, name="proj_kw",
             epilogue=functools.partial(_ep_kw, idx_dim=idim, eps=eps, wscale=(ih * idim) ** -0.5),
             extras=[_lane_row(lw["kidx_norm_g"], 0)])
    u = _mm([h], cols(6, 7), tm=tmm, tn=_pick(cch, COL_TILES), name="proj_u")
    z = _mm([h], cols(7, 8), tm=tmm, tn=_pick(dnw, COL_TILES), name="proj_z")
    ba = _mm([h], cols(8, 10, LANES), tm=tmm, tn=LANES, name="proj_ba")
    gates = _mm([h], cols(10, 12), tm=tmm, tn=_pick(d, COL_TILES), epilogue=_ep_sigmoid, name="proj_g")

    ki_all = kw[:, :idim]
    wi_all = kw[:, idim:idim + ih]
    bias_p = _idx_prompt(qi[:s_len].reshape(s_len * ih, idim), wi_all[:s_len].reshape(s_len * ih, 1),
                         ki_all[:s_len].astype(BF16), s_len=s_len, ih=ih,
                         ktop=min(CFG["index_topk"], s_len // 4),
                         tq=_pick(s_len, (128, 64, 32, 16)), tk=_pick(s_len, (256, 128)))
    oa_p = _attn_prompt(qn, kv, bias_p, s_len=s_len, nh=nh, kvh=kvh, hd=hd,
                        tq=_pick(s_len, (256, 128)), tk=_pick(s_len, (512, 256, 128)))
    n_pages = page_table.shape[1]
    bias_s = _idx_sample(page_table, qi[s_len:].reshape(bd, t * ih, idim),
                         wi_all[s_len:].reshape(bd, t * ih, 1), pool_ki,
                         _pad_rows(ki_all[s_len:].reshape(bd, t, idim), psz, 1),
                         t=t, ih=ih, ktop=min(CFG["index_topk"], (n_pages * psz + t) // 4))
    kv_s = kv[s_len:].reshape(bd, t, 2 * kvw)
    oa_s = _attn_sample(page_table, qn[s_len:].astype(F32).reshape(bd, t, aw),
                        pool_k, pool_v,
                        _pad_rows(kv_s[:, :, :kvw], psz, 1), _pad_rows(kv_s[:, :, kvw:], psz, 1),
                        bias_s, t=t, nh=nh, kvh=kvh, hd=hd)
    o_a = jnp.concatenate([oa_p, oa_s.reshape(n_s, aw).astype(BF16)], axis=0)

    cw8 = _pad_rows(lw["conv_w"], SUBLANES, 0)
    al_row = _lane_row(lw["dn_a_log"], dnh)
    dtb_row = _lane_row(lw["dn_dt_bias"], dnh)
    ng_row = lw["dn_norm_g"][None]
    y_p, bg_p = _dn_prep(u, jnp.zeros((bp, SUBLANES, cch), F32), cw8, ba, al_row, dtb_row,
                         row_off=0, nrows=s_len, tm=_pick(s_len, (128, 64, 32, 16, 8)),
                         seq_len=s_len, nh=dnh, hd=dv)
    y_s, bg_s = _dn_prep(u, _pad_rows(st_conv.astype(F32), SUBLANES, 1, front=True), cw8, ba, al_row,
                         dtb_row, row_off=s_len, nrows=n_s, tm=SUBLANES, seq_len=t, nh=dnh, hd=dv)
    chunk = CFG["dn_chunk"]
    ob_p, sd_p = _dn_scan(y_p, bg_p, z, jnp.zeros((bp,) + st_delta.shape[1:], F32), ng_row,
                          z_row_off=0, c=chunk if s_len % chunk == 0 else s_len, seq_len=s_len,
                          nh=dnh, hd=dv)
    ob_s, sd_s = _dn_scan(y_s, bg_s, z, st_delta.astype(F32), ng_row,
                          z_row_off=s_len, c=chunk if t % chunk == 0 else t, seq_len=t, nh=dnh, hd=dv)
    o_b = jnp.concatenate([ob_p, ob_s], axis=0)

    tnd = _pick(d, COL_TILES)
    ma = _mm([o_a], lw["w_br_a"].astype(BF16), tm=tmm, tn=tnd, name="branch_a",
             epilogue=lambda acc, j, g: acc * g,
             extras=[(gates, (tmm, tnd), lambda i, j, k: (i, j))])
    merged = _mm([o_b], lw["w_br_b"].astype(BF16), tm=tmm, tn=tnd, out_dtype=BF16, name="branch_b",
                 epilogue=lambda acc, j, g, other: acc * g + other,
                 extras=[(gates, (tmm, tnd), lambda i, j, k: (i, j + d // tnd)), ma])
    x1 = _mm([merged], lw["w_out"].astype(BF16), tm=tmod, tn=tnd, name="out_proj",
             epilogue=lambda acc, j, x, g1: x + g1 * acc,
             extras=[x_all, (gate1, (tmod, tnd), lambda i, j, k: (jnp.where(i >= npt, 1, 0), j))])

    h2 = _normmod(x1, lw["norm2_g"][None], scale2, shift2, tmod=tmod, n_prompt_tiles=npt, out_dtype=F32)
    y = _moe(h2, x1, gate2, lw, tmod=tmod, npt=npt, tmm=tmm)

    nb = CFG["conv_w"] - 1
    u_s = u[s_len:].reshape(bd, t, cch)
    state_p = (kv[:s_len, :kvw].reshape(bp, s_len, kvh, hd), kv[:s_len, kvw:].reshape(bp, s_len, kvh, hd),
               ki_all[:s_len].reshape(bp, s_len, idim), u[s_len - nb:s_len][None], sd_p)
    state_s = (kv_s[:, :, :kvw].reshape(bd, t, kvh, hd), kv_s[:, :, kvw:].reshape(bd, t, kvh, hd),
               ki_all[s_len:].reshape(bd, t, idim), u_s[:, t - nb:], sd_s)
    return y[:s_len].reshape(bp, s_len, d), y[s_len:].reshape(bd, t, d), state_p, state_s


_LAYER_WEIGHTS = ("w_ada", "b_ada", "norm1_g", "w_in", "q_norm_g", "k_norm_g", "kidx_norm_g", "conv_w",
                  "dn_a_log", "dn_dt_bias", "dn_norm_g", "w_br_a", "w_br_b", "w_out", "norm2_g",
                  "w_router", "router_bias", "w_e_gate", "w_e_up", "w_e_down", "w_s_gate", "w_s_up",
                  "w_s_down")


def kernel(x_prompt, x_sample, cache_k, cache_v, cache_kidx, state_conv, state_delta, page_table,
           c_prompt, c_sample, w_ada, b_ada, norm1_g, w_in, q_norm_g, k_norm_g, kidx_norm_g, conv_w,
           dn_a_log, dn_dt_bias, dn_norm_g, w_br_a, w_br_b, w_out, norm2_g, w_router, router_bias,
           w_e_gate, w_e_up, w_e_down, w_s_gate, w_s_up, w_s_down):
    weights = dict(zip(_LAYER_WEIGHTS, (w_ada, b_ada, norm1_g, w_in, q_norm_g, k_norm_g, kidx_norm_g,
                                        conv_w, dn_a_log, dn_dt_bias, dn_norm_g, w_br_a, w_br_b, w_out,
                                        norm2_g, w_router, router_bias, w_e_gate, w_e_up, w_e_down,
                                        w_s_gate, w_s_up, w_s_down)))
    y_p, y_s = x_prompt, x_sample
    states_p, states_s = [], []
    for layer in range(w_ada.shape[0]):
        lw = {name: w[layer] for name, w in weights.items()}
        y_p, y_s, st_p, st_s = _layer(y_p, y_s, c_prompt, c_sample, cache_k[layer], cache_v[layer],
                                      cache_kidx[layer], state_conv[layer], state_delta[layer],
                                      page_table, lw)
        states_p.append(st_p)
        states_s.append(st_s)
    stack = lambda sts, i: jnp.stack([st[i] for st in sts])
    return (y_p, y_s) + tuple(stack(states_p, i) for i in range(5)) + tuple(stack(states_s, i) for i in range(5))
```
